```python
import math
import jax, jax.numpy as jnp
from jax import lax
import numpy as np

D_MODEL = 1024
BATCH = 2
SEQ = 8192
DEPTH = 2
DEC_BATCH = 8
DEC_SEQ = 2048
PAST_LEN = 128

D_MIX = D_MODEL
D_A = D_MIX // 2
D_B = D_MIX - D_A
N_HEADS_A = 8
QK_NOPE = 64
QK_ROPE = 32
V_DIM = D_A // N_HEADS_A
Q_LORA = 384
KV_LORA = 256
ROPE_THETA = 10000.0
Q_BLOCK = 128
CHUNK = 128
G_B = 8
C_B = D_B // G_B
D_FF = ((8 * D_MODEL // 3 + 255) // 256) * 256
EPS = 1e-6
IN_COLS = Q_LORA + KV_LORA + QK_ROPE + 2 * D_B
IN_SPLITS = [Q_LORA, Q_LORA + KV_LORA, Q_LORA + KV_LORA + QK_ROPE, Q_LORA + KV_LORA + QK_ROPE + D_B]
N_MOD = 6

kernel_name = 'hybrid_mla_gmlp_encoder'


def _rmsnorm(x, g):
    xf = x.astype(jnp.float32)
    y = xf * lax.rsqrt(jnp.mean(xf * xf, axis=-1, keepdims=True) + EPS)
    return (y * g.astype(jnp.float32)).astype(x.dtype)


def _rope(x, cos, sin):
    half = x.shape[-1] // 2
    x1, x2 = x[..., :half], x[..., half:]
    cos = cos.astype(x.dtype)
    sin = sin.astype(x.dtype)
    return jnp.concatenate([x1 * cos - x2 * sin, x1 * sin + x2 * cos], axis=-1)


def _mla_attention(q_nope, q_rope, k_nope, k_rope, val):
    B, S, H, _ = q_nope.shape
    nb = S // Q_BLOCK
    scale = 1.0 / math.sqrt(QK_NOPE + QK_ROPE)
    qn = q_nope.reshape(B, nb, Q_BLOCK, H, QK_NOPE).transpose(1, 0, 2, 3, 4)
    qr = q_rope.reshape(B, nb, Q_BLOCK, H, QK_ROPE).transpose(1, 0, 2, 3, 4)

    def block(args):
        qn_b, qr_b = args
        s = (jnp.einsum('bqhd,bkhd->bhqk', qn_b, k_nope)
             + jnp.einsum('bqhd,bkd->bhqk', qr_b, k_rope))
        p = jax.nn.softmax(s.astype(jnp.float32) * scale, axis=-1)
        return jnp.einsum('bhqk,bkhd->bqhd', p.astype(val.dtype), val)

    out = lax.map(block, (qn, qr))
    return out.transpose(1, 0, 2, 3, 4).reshape(B, S, H * V_DIM)


def _spatial_gating(u, v, w_s, b_s):
    B, S, _ = v.shape
    n = S // CHUNK
    vr = v.reshape(B, n, CHUNK, G_B, C_B)
    mixed = jnp.einsum('gts,bnsgc->bntgc', w_s, vr) + b_s.T[None, None, :, :, None].astype(v.dtype)
    return u * mixed.reshape(B, S, D_B)


def _mixer(h, w_in, g_q_a, w_q_b, g_kv_a, w_kv_b, g_sgu, w_spatial, b_spatial, g_out_a, g_out_b, w_out):
    B, S, _ = h.shape
    z = h @ w_in
    q_a, kv_a, k_r, u, v = jnp.split(z, IN_SPLITS, axis=-1)
    q = (_rmsnorm(q_a, g_q_a) @ w_q_b).reshape(B, S, N_HEADS_A, QK_NOPE + QK_ROPE)
    q_nope, q_rope = q[..., :QK_NOPE], q[..., QK_NOPE:]
    kv = (_rmsnorm(kv_a, g_kv_a) @ w_kv_b).reshape(B, S, N_HEADS_A, QK_NOPE + V_DIM)
    k_nope, val = kv[..., :QK_NOPE], kv[..., QK_NOPE:]
    pos = jnp.arange(S, dtype=jnp.float32)
    inv_freq = ROPE_THETA ** (-jnp.arange(0, QK_ROPE, 2, dtype=jnp.float32) / QK_ROPE)
    ang = pos[:, None] * inv_freq[None, :]
    cos, sin = jnp.cos(ang), jnp.sin(ang)
    q_rope = _rope(q_rope, cos[:, None, :], sin[:, None, :])
    k_rope = _rope(k_r, cos, sin)
    attn = _mla_attention(q_nope, q_rope, k_nope, k_rope, val)
    sgu = _spatial_gating(jax.nn.gelu(u), _rmsnorm(jax.nn.gelu(v), g_sgu), w_spatial, b_spatial)
    merged = jnp.concatenate([_rmsnorm(attn, g_out_a), _rmsnorm(sgu, g_out_b)], axis=-1)
    return merged @ w_out


def _trunk(x, c, params):
    (w_mod, b_mod, g_pre_mix, g_post_mix, g_pre_ffn, g_post_ffn, w_in, g_q_a, w_q_b, g_kv_a, w_kv_b,
     g_sgu, w_spatial, b_spatial, g_out_a, g_out_b, w_out, w_gate, w_up, w_down) = params
    cs = jax.nn.silu(c)
    for l in range(DEPTH):
        mod = (cs @ w_mod[l] + b_mod[l])[:, None, :]
        sh1, sc1, ga1, sh2, sc2, ga2 = jnp.split(mod, N_MOD, axis=-1)
        h = _rmsnorm(x, g_pre_mix[l]) * (1 + sc1) + sh1
        m = _mixer(h, w_in[l], g_q_a[l], w_q_b[l], g_kv_a[l], w_kv_b[l], g_sgu[l], w_spatial[l], b_spatial[l],
                   g_out_a[l], g_out_b[l], w_out[l])
        x = x + ga1 * _rmsnorm(m, g_post_mix[l])
        h = _rmsnorm(x, g_pre_ffn[l]) * (1 + sc2) + sh2
        f = (jax.nn.silu(h @ w_gate[l]) * (h @ w_up[l])) @ w_down[l]
        x = x + ga2 * _rmsnorm(f, g_post_ffn[l])
    return x


def setup_inputs(seed: int = 0) -> dict:
    key = jax.random.key(seed)
    ks = jax.random.split(key, 32)
    f32 = jnp.float32

    def nrm(k, shape, scale):
        return jax.random.normal(k, shape, f32) * scale

    def gain(k, shape):
        return 1.0 + 0.02 * jax.random.normal(k, shape, f32)

    L = DEPTH
    return {
        'x_prompt': nrm(ks[0], (BATCH, SEQ, D_MODEL), 1.0),
        'x_sample': nrm(ks[1], (DEC_BATCH, DEC_SEQ, D_MODEL), 1.0),
        'c_prompt': nrm(ks[2], (BATCH, D_MODEL), 1.0),
        'c_sample': nrm(ks[3], (DEC_BATCH, D_MODEL), 1.0),
        'w_mod': nrm(ks[4], (L, D_MODEL, N_MOD * D_MODEL), 0.5 * D_MODEL ** -0.5),
        'b_mod': nrm(ks[5], (L, N_MOD * D_MODEL), 0.01),
        'g_pre_mix': gain(ks[6], (L, D_MODEL)),
        'g_post_mix': gain(ks[7], (L, D_MODEL)),
        'g_pre_ffn': gain(ks[8], (L, D_MODEL)),
        'g_post_ffn': gain(ks[9], (L, D_MODEL)),
        'w_in': nrm(ks[10], (L, D_MODEL, IN_COLS), D_MODEL ** -0.5),
        'g_q_a': gain(ks[11], (L, Q_LORA)),
        'w_q_b': nrm(ks[12], (L, Q_LORA, N_HEADS_A * (QK_NOPE + QK_ROPE)), Q_LORA ** -0.5),
        'g_kv_a': gain(ks[13], (L, KV_LORA)),
        'w_kv_b': nrm(ks[14], (L, KV_LORA, N_HEADS_A * (QK_NOPE + V_DIM)), KV_LORA ** -0.5),
        'g_sgu': gain(ks[15], (L, D_B)),
        'w_spatial': nrm(ks[16], (L, G_B, CHUNK, CHUNK), CHUNK ** -0.5),
        'b_spatial': 1.0 + nrm(ks[17], (L, G_B, CHUNK), 0.01),
        'g_out_a': gain(ks[18], (L, D_A)),
        'g_out_b': gain(ks[19], (L, D_B)),
        'w_out': nrm(ks[20], (L, D_MIX, D_MODEL), D_MIX ** -0.5),
        'w_gate': nrm(ks[21], (L, D_MODEL, D_FF), D_MODEL ** -0.5),
        'w_up': nrm(ks[22], (L, D_MODEL, D_FF), D_MODEL ** -0.5),
        'w_down': nrm(ks[23], (L, D_FF, D_MODEL), D_FF ** -0.5),
    }


def reference(x_prompt, x_sample, c_prompt, c_sample, w_mod, b_mod, g_pre_mix, g_post_mix, g_pre_ffn, g_post_ffn,
              w_in, g_q_a, w_q_b, g_kv_a, w_kv_b, g_sgu, w_spatial, b_spatial, g_out_a, g_out_b, w_out,
              w_gate, w_up, w_down):
    params = (w_mod, b_mod, g_pre_mix, g_post_mix, g_pre_ffn, g_post_ffn, w_in, g_q_a, w_q_b, g_kv_a, w_kv_b,
              g_sgu, w_spatial, b_spatial, g_out_a, g_out_b, w_out, w_gate, w_up, w_down)
    y_prompt = _trunk(x_prompt, c_prompt, params)
    y_sample = _trunk(x_sample, c_sample, params)
    return (y_prompt, y_sample)
```

```python
import functools
import math

import jax
import jax.numpy as jnp
from jax import lax
from jax.experimental import pallas as pl
from jax.experimental.pallas import tpu as pltpu

D_MODEL = 1024
DEPTH = 2
N_HEADS = 8
QK_NOPE = 64
QK_ROPE = 32
ROPE_HALF = QK_ROPE // 2
V_DIM = 64
Q_LORA = 384
KV_LORA = 256
D_A = N_HEADS * V_DIM
D_B = 512
G_B = 8
C_B = D_B // G_B
CHUNK = 128
D_FF = 2816
EPS = 1e-6
N_MOD = 6
ROPE_THETA = 10000.0

LANES = 128
SLAB = LANES
HEADS_PER_STEP = 2
IN_COLS_PADDED = Q_LORA + KV_LORA + 2 * D_B + SLAB
FF_SPLIT = 2
VMEM_LIMIT = 56 * 1024 * 1024

BF16 = jnp.bfloat16
F32 = jnp.float32
NEG_BIG = -1e30


def _token_tile(seq):
    return min(seq, 512)


def _rms(x, g):
    r = lax.rsqrt(jnp.mean(x * x, axis=-1, keepdims=True) + EPS)
    return x * r * g


def _dot(a, b):
    return jnp.dot(a, b, preferred_element_type=F32)


def _mod_kernel(c_ref, w_ref, b_ref, o_ref):
    cs = jax.nn.silu(c_ref[...])
    o_ref[0] = _dot(cs.astype(BF16), w_ref[0].astype(BF16)) + b_ref[0]


def _modulation(c_all, w_mod, b_mod):
    rows = c_all.shape[0]
    n = w_mod.shape[-1]
    tn = 1536
    return pl.pallas_call(
        _mod_kernel,
        out_shape=jax.ShapeDtypeStruct((DEPTH, rows, n), F32),
        grid=(DEPTH, n // tn),
        in_specs=[
            pl.BlockSpec((rows, D_MODEL), lambda l, j: (0, 0)),
            pl.BlockSpec((1, D_MODEL, tn), lambda l, j: (l, 0, j)),
            pl.BlockSpec((1, 1, tn), lambda l, j: (l, 0, j)),
        ],
        out_specs=pl.BlockSpec((1, rows, tn), lambda l, j: (l, 0, j)),
        compiler_params=pltpu.CompilerParams(
            dimension_semantics=("arbitrary", "arbitrary"), vmem_limit_bytes=VMEM_LIMIT),
        name="adaln_modulation",
    )(c_all, w_mod, b_mod.reshape(DEPTH, 1, n))


def _rope_slab(x, c, s_lo, s_hi):
    return x * c + pltpu.roll(x, SLAB - ROPE_HALF, 1) * s_lo + pltpu.roll(x, ROPE_HALF, 1) * s_hi


def _premix_kernel(x_ref, mod_ref, gpre_ref, win_ref, gq_ref, wq_ref, gkv_ref, wkv_ref, gsgu_ref, ws_ref,
                   bs_ref, goutb_ref, rope_ref, q_out, k_out, v_out, sg_out, mixed_sc):
    x = x_ref[0]
    shift = mod_ref[0, :, 0:D_MODEL]
    scale = mod_ref[0, :, D_MODEL:2 * D_MODEL]
    h = _rms(x, gpre_ref[...]) * (1.0 + scale) + shift
    z = _dot(h.astype(BF16), win_ref[...])

    o_kv, o_u, o_v, o_kr = Q_LORA, Q_LORA + KV_LORA, Q_LORA + KV_LORA + D_B, Q_LORA + KV_LORA + 2 * D_B
    cq, sq_lo, sq_hi = rope_ref[0], rope_ref[1], rope_ref[2]
    ck, sk_lo, sk_hi = rope_ref[3], rope_ref[4], rope_ref[5]

    qn = _rms(z[:, 0:Q_LORA], gq_ref[...]).astype(BF16)
    q = _dot(qn, wq_ref[...])
    for hd in range(N_HEADS):
        sl = slice(hd * SLAB, (hd + 1) * SLAB)
        q_out[0, :, sl] = _rope_slab(q[:, sl], cq, sq_lo, sq_hi).astype(BF16)

    kvn = _rms(z[:, o_kv:o_u], gkv_ref[...]).astype(BF16)
    kv = _dot(kvn, wkv_ref[...])
    kr = _rope_slab(z[:, o_kr:o_kr + SLAB], ck, sk_lo, sk_hi)
    lane = lax.broadcasted_iota(jnp.int32, (1, SLAB), 1)
    ones_col = (lane == V_DIM).astype(F32)
    for hd in range(N_HEADS):
        sl = slice(hd * SLAB, (hd + 1) * SLAB)
        k_out[0, :, sl] = (kv[:, sl] + kr).astype(BF16)
        vsl = slice(N_HEADS * SLAB + hd * SLAB, N_HEADS * SLAB + (hd + 1) * SLAB)
        v_out[0, :, sl] = (kv[:, vsl] + ones_col).astype(BF16)

    gu = jax.nn.gelu(z[:, o_u:o_v])
    vn = _rms(jax.nn.gelu(z[:, o_v:o_kr]), gsgu_ref[...])
    low = lax.broadcasted_iota(jnp.int32, (CHUNK, SLAB), 1) < C_B
    n_chunks = x.shape[0] // CHUNK
    for c in range(n_chunks):
        rows = slice(c * CHUNK, (c + 1) * CHUNK)
        for j in range(D_B // SLAB):
            cols = slice(j * SLAB, (j + 1) * SLAB)
            slab = vn[rows, cols]
            rhs = jnp.concatenate([jnp.where(low, slab, 0.0), jnp.where(low, 0.0, slab)], axis=0).astype(BF16)
            mixed_sc[rows, cols] = _dot(ws_ref[j], rhs) + bs_ref[:, cols]
    sgu = gu * mixed_sc[...]
    sg_out[0] = _rms(sgu, goutb_ref[...]).astype(BF16)


def _premix(x, mod, p, rope):
    b, s, _ = x.shape
    t = _token_tile(s)
    const2 = lambda bi, i: (0, 0)
    const3 = lambda bi, i: (0, 0, 0)
    tok = lambda bi, i: (bi, i, 0)
    n_slab = N_HEADS * SLAB
    return pl.pallas_call(
        _premix_kernel,
        out_shape=(
            jax.ShapeDtypeStruct((b, s, n_slab), BF16),
            jax.ShapeDtypeStruct((b, s, n_slab), BF16),
            jax.ShapeDtypeStruct((b, s, n_slab), BF16),
            jax.ShapeDtypeStruct((b, s, D_B), BF16),
        ),
        grid=(b, s // t),
        in_specs=[
            pl.BlockSpec((1, t, D_MODEL), tok),
            pl.BlockSpec((1, 1, N_MOD * D_MODEL), lambda bi, i: (bi, 0, 0)),
            pl.BlockSpec((1, D_MODEL), const2),
            pl.BlockSpec((D_MODEL, IN_COLS_PADDED), const2),
            pl.BlockSpec((1, Q_LORA), const2),
            pl.BlockSpec((Q_LORA, n_slab), const2),
            pl.BlockSpec((1, KV_LORA), const2),
            pl.BlockSpec((KV_LORA, 2 * n_slab), const2),
            pl.BlockSpec((1, D_B), const2),
            pl.BlockSpec((D_B // SLAB, CHUNK, 2 * CHUNK), const3),
            pl.BlockSpec((CHUNK, D_B), const2),
            pl.BlockSpec((1, D_B), const2),
            pl.BlockSpec((6, t, SLAB), lambda bi, i: (0, i, 0)),
        ],
        out_specs=(
            pl.BlockSpec((1, t, n_slab), tok),
            pl.BlockSpec((1, t, n_slab), tok),
            pl.BlockSpec((1, t, n_slab), tok),
            pl.BlockSpec((1, t, D_B), tok),
        ),
        scratch_shapes=[pltpu.VMEM((t, D_B), F32)],
        compiler_params=pltpu.CompilerParams(
            dimension_semantics=("arbitrary", "arbitrary"), vmem_limit_bytes=VMEM_LIMIT),
        name="premix",
    )(x, mod, p["g_pre_mix"], p["w_in"], p["g_q_a"], p["w_q"], p["g_kv_a"], p["w_kv"], p["g_sgu"],
      p["w_spatial"], p["b_spatial"], p["g_out_b"], rope)


def _attn_kernel(q_ref, k_ref, v_ref, o_ref, m_sc, acc_sc, *, tk):
    tq = q_ref.shape[1]
    n_chunks = k_ref.shape[1] // tk
    m_sc[...] = jnp.full(m_sc.shape, NEG_BIG, F32)
    acc_sc[...] = jnp.zeros(acc_sc.shape, F32)

    def body(j, carry):
        start = pl.multiple_of(j * tk, tk)
        for hd in range(HEADS_PER_STEP):
            sl = slice(hd * SLAB, (hd + 1) * SLAB)
            q = q_ref[0, :, sl]
            k = k_ref[0, pl.ds(start, tk), sl]
            v = v_ref[0, pl.ds(start, tk), sl]
            s = lax.dot_general(q, k, (((1,), (1,)), ((), ())), preferred_element_type=F32)
            m_prev = m_sc[hd]
            m_next = jnp.maximum(m_prev, jnp.max(s, axis=1, keepdims=True))
            p = jnp.exp2(s - jnp.tile(m_next, (1, tk // LANES)))
            alpha = jnp.exp2(m_prev - m_next)
            acc_sc[hd] = acc_sc[hd] * alpha + _dot(p.astype(BF16), v)
            m_sc[hd] = m_next
        return carry

    lax.fori_loop(0, n_chunks, body, 0)
    outs = []
    for hd in range(HEADS_PER_STEP):
        acc = acc_sc[hd]
        outs.append(acc[:, 0:V_DIM] / acc[:, V_DIM:V_DIM + 1])
    o_ref[0] = jnp.concatenate(outs, axis=1).astype(o_ref.dtype)


def _attention(q, k, v):
    b, s, _ = q.shape
    tq = min(s, 512)
    tk = min(s, 512)
    w = HEADS_PER_STEP * SLAB
    return pl.pallas_call(
        functools.partial(_attn_kernel, tk=tk),
        out_shape=jax.ShapeDtypeStruct((b, s, D_A), BF16),
        grid=(b, N_HEADS // HEADS_PER_STEP, s // tq),
        in_specs=[
            pl.BlockSpec((1, tq, w), lambda bi, hp, i: (bi, i, hp)),
            pl.BlockSpec((1, s, w), lambda bi, hp, i: (bi, 0, hp)),
            pl.BlockSpec((1, s, w), lambda bi, hp, i: (bi, 0, hp)),
        ],
        out_specs=pl.BlockSpec((1, tq, HEADS_PER_STEP * V_DIM), lambda bi, hp, i: (bi, i, hp)),
        scratch_shapes=[
            pltpu.VMEM((HEADS_PER_STEP, tq, LANES), F32),
            pltpu.VMEM((HEADS_PER_STEP, tq, SLAB), F32),
        ],
        compiler_params=pltpu.CompilerParams(
            dimension_semantics=("arbitrary", "arbitrary", "arbitrary"), vmem_limit_bytes=VMEM_LIMIT),
        name="mla_attention",
    )(q, k, v)


def _postmix_kernel(x_ref, mod_ref, attn_ref, sg_ref, gouta_ref, wout_ref, gpost_ref, gpre_ref, wgate_ref,
                    wup_ref, wdown_ref, gpostf_ref, o_ref):
    x = x_ref[0]
    gate1 = mod_ref[0, :, 2 * D_MODEL:3 * D_MODEL]
    shift2 = mod_ref[0, :, 3 * D_MODEL:4 * D_MODEL]
    scale2 = mod_ref[0, :, 4 * D_MODEL:5 * D_MODEL]
    gate2 = mod_ref[0, :, 5 * D_MODEL:6 * D_MODEL]

    a = _rms(attn_ref[0].astype(F32), gouta_ref[...]).astype(BF16)
    merged = jnp.concatenate([a, sg_ref[0]], axis=1)
    m = _dot(merged, wout_ref[...])
    x1 = x + gate1 * _rms(m, gpost_ref[...])

    h = (_rms(x1, gpre_ref[...]) * (1.0 + scale2) + shift2).astype(BF16)
    piece = D_FF // FF_SPLIT
    f = None
    for c in range(FF_SPLIT):
        cols = slice(c * piece, (c + 1) * piece)
        act = (jax.nn.silu(_dot(h, wgate_ref[:, cols])) * _dot(h, wup_ref[:, cols])).astype(BF16)
        part = _dot(act, wdown_ref[cols, :])
        f = part if f is None else f + part
    o_ref[0] = x1 + gate2 * _rms(f, gpostf_ref[...])


def _postmix(x, mod, attn, sg, p):
    b, s, _ = x.shape
    t = _token_tile(s)
    const2 = lambda bi, i: (0, 0)
    tok = lambda bi, i: (bi, i, 0)
    resident = functools.partial(pl.BlockSpec, index_map=const2, pipeline_mode=pl.Buffered(1))
    return pl.pallas_call(
        _postmix_kernel,
        out_shape=jax.ShapeDtypeStruct(x.shape, F32),
        grid=(b, s // t),
        in_specs=[
            pl.BlockSpec((1, t, D_MODEL), tok),
            pl.BlockSpec((1, 1, N_MOD * D_MODEL), lambda bi, i: (bi, 0, 0)),
            pl.BlockSpec((1, t, D_A), tok),
            pl.BlockSpec((1, t, D_B), tok),
            pl.BlockSpec((1, D_A), const2),
            resident((D_A + D_B, D_MODEL)),
            pl.BlockSpec((1, D_MODEL), const2),
            pl.BlockSpec((1, D_MODEL), const2),
            resident((D_MODEL, D_FF)),
            resident((D_MODEL, D_FF)),
            resident((D_FF, D_MODEL)),
            pl.BlockSpec((1, D_MODEL), const2),
        ],
        out_specs=pl.BlockSpec((1, t, D_MODEL), tok),
        compiler_params=pltpu.CompilerParams(
            dimension_semantics=("arbitrary", "arbitrary"), vmem_limit_bytes=VMEM_LIMIT),
        name="postmix",
    )(x, mod, attn, sg, p["g_out_a"], p["w_out"], p["g_post_mix"], p["g_pre_ffn"], p["w_gate"], p["w_up"],
      p["w_down"], p["g_post_ffn"])


def _pad_heads(w, width):
    k = w.shape[0]
    w = w.reshape(k, N_HEADS, width)
    return jnp.pad(w, ((0, 0), (0, 0), (0, SLAB - width))).reshape(k, N_HEADS * SLAB)


def _layer_params(l, w_in, g_q_a, w_q_b, g_kv_a, w_kv_b, g_sgu, w_spatial, b_spatial, g_out_a, g_out_b, w_out,
                  g_pre_mix, g_post_mix, g_pre_ffn, g_post_ffn, w_gate, w_up, w_down):
    o_kr = Q_LORA + KV_LORA
    o_u = o_kr + QK_ROPE
    wi = w_in[l]
    kr_slab = jnp.pad(wi[:, o_kr:o_u], ((0, 0), (QK_NOPE, SLAB - QK_NOPE - QK_ROPE)))
    w_in_p = jnp.concatenate([wi[:, :o_kr], wi[:, o_u:], kr_slab], axis=1)
    kvw = w_kv_b[l].reshape(KV_LORA, N_HEADS, QK_NOPE + V_DIM)
    w_k = _pad_heads(kvw[:, :, :QK_NOPE].reshape(KV_LORA, N_HEADS * QK_NOPE), QK_NOPE)
    w_v = _pad_heads(kvw[:, :, QK_NOPE:].reshape(KV_LORA, N_HEADS * V_DIM), V_DIM)
    ws = w_spatial[l]
    ws_pairs = jnp.concatenate([ws[0::2], ws[1::2]], axis=2)
    bias = jnp.repeat(b_spatial[l].T, C_B, axis=1)
    row = lambda g: g[l].reshape(1, -1)
    return dict(
        g_pre_mix=row(g_pre_mix), w_in=w_in_p.astype(BF16), g_q_a=row(g_q_a),
        w_q=_pad_heads(w_q_b[l], QK_NOPE + QK_ROPE).astype(BF16), g_kv_a=row(g_kv_a),
        w_kv=jnp.concatenate([w_k, w_v], axis=1).astype(BF16), g_sgu=row(g_sgu),
        w_spatial=ws_pairs.astype(BF16), b_spatial=bias, g_out_b=row(g_out_b), g_out_a=row(g_out_a),
        w_out=w_out[l].astype(BF16), g_post_mix=row(g_post_mix), g_pre_ffn=row(g_pre_ffn),
        w_gate=w_gate[l].astype(BF16), w_up=w_up[l].astype(BF16), w_down=w_down[l].astype(BF16),
        g_post_ffn=row(g_post_ffn))


def _rope_tables(seq):
    pos = jnp.arange(seq, dtype=F32)
    inv_freq = ROPE_THETA ** (-jnp.arange(0, QK_ROPE, 2, dtype=F32) / QK_ROPE)
    ang = pos[:, None] * inv_freq[None, :]
    cos, sin = jnp.cos(ang), jnp.sin(ang)
    zeros = lambda n: jnp.zeros((seq, n), F32)
    tail = SLAB - QK_NOPE - QK_ROPE
    c = jnp.concatenate([jnp.ones((seq, QK_NOPE), F32), cos, cos, zeros(tail)], axis=1)
    s_lo = jnp.concatenate([zeros(QK_NOPE), -sin, zeros(ROPE_HALF + tail)], axis=1)
    s_hi = jnp.concatenate([zeros(QK_NOPE + ROPE_HALF), sin, zeros(tail)], axis=1)
    k_tabs = jnp.stack([c, s_lo, s_hi])
    q_scale = math.log2(math.e) / math.sqrt(QK_NOPE + QK_ROPE)
    return jnp.concatenate([k_tabs * q_scale, k_tabs], axis=0)


def _trunk(x, mod, layers):
    rope = _rope_tables(x.shape[1])
    for l in range(DEPTH):
        mod_l = mod[l][:, None, :]
        q, k, v, sg = _premix(x, mod_l, layers[l], rope)
        attn = _attention(q, k, v)
        x = _postmix(x, mod_l, attn, sg, layers[l])
    return x


def kernel(x_prompt, x_sample, c_prompt, c_sample, w_mod, b_mod, g_pre_mix, g_post_mix, g_pre_ffn, g_post_ffn,
           w_in, g_q_a, w_q_b, g_kv_a, w_kv_b, g_sgu, w_spatial, b_spatial, g_out_a, g_out_b, w_out,
           w_gate, w_up, w_down):
    layers = [
        _layer_params(l, w_in, g_q_a, w_q_b, g_kv_a, w_kv_b, g_sgu, w_spatial, b_spatial, g_out_a, g_out_b, w_out,
                      g_pre_mix, g_post_mix, g_pre_ffn, g_post_ffn, w_gate, w_up, w_down)
        for l in range(DEPTH)
    ]
    nb_p, nb_s = c_prompt.shape[0], c_sample.shape[0]
    rows = -(-(nb_p + nb_s) // 8) * 8
    c_all = jnp.concatenate([c_prompt, c_sample, jnp.zeros((rows - nb_p - nb_s, D_MODEL), F32)], axis=0)
    mod = _modulation(c_all, w_mod, b_mod)
    y_prompt = _trunk(x_prompt, mod[:, :nb_p], layers)
    y_sample = _trunk(x_sample, mod[:, nb_p:nb_p + nb_s], layers)
    return (y_prompt, y_sample)
```

```python
import functools
import math

import jax
import jax.numpy as jnp
from jax import lax
from jax.experimental import pallas as pl
from jax.experimental.pallas import tpu as pltpu

D_MODEL = 1024
DEPTH = 2
N_HEADS = 8
QK_NOPE = 64
QK_ROPE = 32
ROPE_HALF = QK_ROPE // 2
V_DIM = 64
Q_LORA = 384
KV_LORA = 256
D_A = N_HEADS * V_DIM
D_B = 512
G_B = 8
C_B = D_B // G_B
CHUNK = 128
D_FF = 2816
EPS = 1e-6
N_MOD = 6
ROPE_THETA = 10000.0

LANES = 128
SLAB = LANES
HEADS_PER_STEP = 2
Q_TILE = 512
KV_CHUNK = 512
IN_COLS_PADDED = Q_LORA + KV_LORA + 2 * D_B + SLAB
FF_SPLIT = 2
VMEM_LIMIT = 56 * 1024 * 1024

BF16 = jnp.bfloat16
F32 = jnp.float32
NEG_BIG = -1e30
Q_SCALE = math.log2(math.e) / math.sqrt(QK_NOPE + QK_ROPE)


def _token_tile(seq):
    return min(seq, 512)


def _rms(x, g):
    r = lax.rsqrt(jnp.mean(x * x, axis=-1, keepdims=True) + EPS)
    return x * r * g


def _dot(a, b):
    return jnp.dot(a, b, preferred_element_type=F32)


def _dot_nt(a, b):
    return lax.dot_general(a, b, (((1,), (1,)), ((), ())), preferred_element_type=F32)


def _mod_kernel(c_ref, w_ref, b_ref, o_ref):
    cs = jax.nn.silu(c_ref[...])
    o_ref[0] = _dot(cs.astype(BF16), w_ref[0].astype(BF16)) + b_ref[0]


def _modulation(c_all, w_mod, b_mod):
    rows = c_all.shape[0]
    n = w_mod.shape[-1]
    tn = 1536
    return pl.pallas_call(
        _mod_kernel,
        out_shape=jax.ShapeDtypeStruct((DEPTH, rows, n), F32),
        grid=(DEPTH, n // tn),
        in_specs=[
            pl.BlockSpec((rows, D_MODEL), lambda l, j: (0, 0)),
            pl.BlockSpec((1, D_MODEL, tn), lambda l, j: (l, 0, j)),
            pl.BlockSpec((1, 1, tn), lambda l, j: (l, 0, j)),
        ],
        out_specs=pl.BlockSpec((1, rows, tn), lambda l, j: (l, 0, j)),
        compiler_params=pltpu.CompilerParams(
            dimension_semantics=("arbitrary", "arbitrary"), vmem_limit_bytes=VMEM_LIMIT),
        name="adaln_modulation",
    )(c_all, w_mod, b_mod.reshape(DEPTH, 1, n))


def _rope_slab(x, c, s_lo, s_hi):
    return x * c + pltpu.roll(x, SLAB - ROPE_HALF, 1) * s_lo + pltpu.roll(x, ROPE_HALF, 1) * s_hi


def _premix_kernel(x_ref, mod_ref, gpre_ref, win_ref, gq_ref, wqt_ref, gkv_ref, wk_ref, wvt_ref, gsgu_ref, ws_ref,
                   bs_ref, goutb_ref, ropek_ref, ropeq_ref, qt_out, k_out, vt_out, sg_out, mixed_sc):
    x = x_ref[0]
    shift = mod_ref[0, :, 0:D_MODEL]
    scale = mod_ref[0, :, D_MODEL:2 * D_MODEL]
    h = _rms(x, gpre_ref[...]) * (1.0 + scale) + shift
    z = _dot(h.astype(BF16), win_ref[...])

    o_kv, o_u, o_v, o_kr = Q_LORA, Q_LORA + KV_LORA, Q_LORA + KV_LORA + D_B, Q_LORA + KV_LORA + 2 * D_B

    qn = _rms(z[:, 0:Q_LORA], gq_ref[...]).astype(BF16)
    qt = _dot_nt(wqt_ref[...], qn)
    cos_t, sin_t = ropeq_ref[0], ropeq_ref[1]
    lo0, hi0, end = QK_NOPE, QK_NOPE + ROPE_HALF, QK_NOPE + QK_ROPE
    for hd in range(N_HEADS):
        base = hd * SLAB
        x_lo = qt[base + lo0:base + hi0]
        x_hi = qt[base + hi0:base + end]
        slab = jnp.concatenate([
            qt[base:base + lo0] * Q_SCALE,
            (x_lo * cos_t - x_hi * sin_t) * Q_SCALE,
            (x_lo * sin_t + x_hi * cos_t) * Q_SCALE,
            qt[base + end:base + SLAB],
        ], axis=0)
        qt_out[0, base:base + SLAB, :] = slab.astype(BF16)

    kvn = _rms(z[:, o_kv:o_u], gkv_ref[...]).astype(BF16)
    kk = _dot(kvn, wk_ref[...])
    kr = _rope_slab(z[:, o_kr:o_kr + SLAB], ropek_ref[0], ropek_ref[1], ropek_ref[2])
    for hd in range(N_HEADS):
        sl = slice(hd * SLAB, (hd + 1) * SLAB)
        k_out[0, :, sl] = (kk[:, sl] + kr).astype(BF16)
    vt = _dot_nt(wvt_ref[...], kvn)
    ones_row = (lax.broadcasted_iota(jnp.int32, (SLAB, 1), 0) == V_DIM).astype(F32)
    for hd in range(N_HEADS):
        base = hd * SLAB
        vt_out[0, base:base + SLAB, :] = (vt[base:base + SLAB] + ones_row).astype(BF16)

    gu = jax.nn.gelu(z[:, o_u:o_v])
    vn = _rms(jax.nn.gelu(z[:, o_v:o_kr]), gsgu_ref[...])
    low = lax.broadcasted_iota(jnp.int32, (CHUNK, SLAB), 1) < C_B
    n_chunks = x.shape[0] // CHUNK
    for c in range(n_chunks):
        rows = slice(c * CHUNK, (c + 1) * CHUNK)
        for j in range(D_B // SLAB):
            cols = slice(j * SLAB, (j + 1) * SLAB)
            slab = vn[rows, cols]
            rhs = jnp.concatenate([jnp.where(low, slab, 0.0), jnp.where(low, 0.0, slab)], axis=0).astype(BF16)
            mixed_sc[rows, cols] = _dot(ws_ref[j], rhs) + bs_ref[:, cols]
    sgu = gu * mixed_sc[...]
    sg_out[0] = _rms(sgu, goutb_ref[...]).astype(BF16)


def _premix(x, mod, p, rope_k, rope_q):
    b, s, _ = x.shape
    t = _token_tile(s)
    const2 = lambda bi, i: (0, 0)
    const3 = lambda bi, i: (0, 0, 0)
    tok = lambda bi, i: (bi, i, 0)
    tok_t = lambda bi, i: (bi, 0, i)
    n_slab = N_HEADS * SLAB
    return pl.pallas_call(
        _premix_kernel,
        out_shape=(
            jax.ShapeDtypeStruct((b, n_slab, s), BF16),
            jax.ShapeDtypeStruct((b, s, n_slab), BF16),
            jax.ShapeDtypeStruct((b, n_slab, s), BF16),
            jax.ShapeDtypeStruct((b, s, D_B), BF16),
        ),
        grid=(b, s // t),
        in_specs=[
            pl.BlockSpec((1, t, D_MODEL), tok),
            pl.BlockSpec((1, 1, N_MOD * D_MODEL), lambda bi, i: (bi, 0, 0)),
            pl.BlockSpec((1, D_MODEL), const2),
            pl.BlockSpec((D_MODEL, IN_COLS_PADDED), const2),
            pl.BlockSpec((1, Q_LORA), const2),
            pl.BlockSpec((n_slab, Q_LORA), const2),
            pl.BlockSpec((1, KV_LORA), const2),
            pl.BlockSpec((KV_LORA, n_slab), const2),
            pl.BlockSpec((n_slab, KV_LORA), const2),
            pl.BlockSpec((1, D_B), const2),
            pl.BlockSpec((D_B // SLAB, CHUNK, 2 * CHUNK), const3),
            pl.BlockSpec((CHUNK, D_B), const2),
            pl.BlockSpec((1, D_B), const2),
            pl.BlockSpec((3, t, SLAB), lambda bi, i: (0, i, 0)),
            pl.BlockSpec((2, ROPE_HALF, t), lambda bi, i: (0, 0, i)),
        ],
        out_specs=(
            pl.BlockSpec((1, n_slab, t), tok_t),
            pl.BlockSpec((1, t, n_slab), tok),
            pl.BlockSpec((1, n_slab, t), tok_t),
            pl.BlockSpec((1, t, D_B), tok),
        ),
        scratch_shapes=[pltpu.VMEM((t, D_B), F32)],
        compiler_params=pltpu.CompilerParams(
            dimension_semantics=("arbitrary", "arbitrary"), vmem_limit_bytes=VMEM_LIMIT),
        name="premix",
    )(x, mod, p["g_pre_mix"], p["w_in"], p["g_q_a"], p["w_qt"], p["g_kv_a"], p["w_k"], p["w_vt"], p["g_sgu"],
      p["w_spatial"], p["b_spatial"], p["g_out_b"], rope_k, rope_q)


def _attn_kernel(qt_ref, k_ref, vt_ref, o_ref, st0, st1, mx0, mx1, p0, p1, al0, al1, m_sc, acc_sc, *, tk):
    n_chunks = k_ref.shape[1] // tk
    heads = range(HEADS_PER_STEP)
    st_sc, mx_sc, p_sc, alpha_sc = (st0, st1), (mx0, mx1), (p0, p1), (al0, al1)

    def scores(c, par):
        start = pl.multiple_of(c * tk, tk)
        for hd in heads:
            sl = slice(hd * SLAB, (hd + 1) * SLAB)
            st = _dot(k_ref[0, pl.ds(start, tk), sl], qt_ref[0, sl, :])
            st_sc[par][hd] = st
            mx_sc[par][hd] = jnp.max(st, axis=0, keepdims=True)

    def probs(par):
        for hd in heads:
            m_prev = m_sc[hd]
            m_next = jnp.maximum(m_prev, mx_sc[par][hd])
            p_sc[par][hd] = jnp.exp2(st_sc[par][hd] - m_next).astype(BF16)
            alpha_sc[par][hd] = jnp.exp2(m_prev - m_next)
            m_sc[hd] = m_next

    def accum(c, par):
        start = pl.multiple_of(c * tk, tk)
        for hd in heads:
            sl = slice(hd * SLAB, (hd + 1) * SLAB)
            pv = _dot(vt_ref[0, sl, pl.ds(start, tk)], p_sc[par][hd])
            acc_sc[hd] = acc_sc[hd] * alpha_sc[par][hd] + pv

    m_sc[...] = jnp.full(m_sc.shape, NEG_BIG, F32)
    acc_sc[...] = jnp.zeros(acc_sc.shape, F32)
    scores(0, 0)
    scores(1, 1)
    probs(0)

    def body(c, carry):
        for par in range(2):
            @pl.when(c % 2 == par)
            def _():
                scores(c + 2, par)
                probs(1 - par)
                accum(c, par)
        return carry

    lax.fori_loop(0, n_chunks - 2, body, 0)
    accum(n_chunks - 2, 0)
    probs(1)
    accum(n_chunks - 1, 1)
    outs = []
    for hd in range(HEADS_PER_STEP):
        acc = acc_sc[hd]
        outs.append(acc[0:V_DIM] / acc[V_DIM:V_DIM + 1])
    o_ref[0] = jnp.concatenate(outs, axis=0).T.astype(o_ref.dtype)


def _attention(qt, k, vt):
    b, s, _ = k.shape
    tq = min(s, Q_TILE)
    tk = min(s // 2, KV_CHUNK)
    assert s % tq == 0 and s % (2 * tk) == 0 and tk % LANES == 0
    w = HEADS_PER_STEP * SLAB
    return pl.pallas_call(
        functools.partial(_attn_kernel, tk=tk),
        out_shape=jax.ShapeDtypeStruct((b, s, D_A), BF16),
        grid=(b, N_HEADS // HEADS_PER_STEP, s // tq),
        in_specs=[
            pl.BlockSpec((1, w, tq), lambda bi, hp, i: (bi, hp, i)),
            pl.BlockSpec((1, s, w), lambda bi, hp, i: (bi, 0, hp)),
            pl.BlockSpec((1, w, s), lambda bi, hp, i: (bi, hp, 0)),
        ],
        out_specs=pl.BlockSpec((1, tq, HEADS_PER_STEP * V_DIM), lambda bi, hp, i: (bi, i, hp)),
        scratch_shapes=[
            pltpu.VMEM((HEADS_PER_STEP, tk, tq), F32),
            pltpu.VMEM((HEADS_PER_STEP, tk, tq), F32),
            pltpu.VMEM((HEADS_PER_STEP, 1, tq), F32),
            pltpu.VMEM((HEADS_PER_STEP, 1, tq), F32),
            pltpu.VMEM((HEADS_PER_STEP, tk, tq), BF16),
            pltpu.VMEM((HEADS_PER_STEP, tk, tq), BF16),
            pltpu.VMEM((HEADS_PER_STEP, 1, tq), F32),
            pltpu.VMEM((HEADS_PER_STEP, 1, tq), F32),
            pltpu.VMEM((HEADS_PER_STEP, 1, tq), F32),
            pltpu.VMEM((HEADS_PER_STEP, SLAB, tq), F32),
        ],
        compiler_params=pltpu.CompilerParams(
            dimension_semantics=("arbitrary", "arbitrary", "arbitrary"), vmem_limit_bytes=VMEM_LIMIT,
        ),
        name="mla_attention",
    )(qt, k, vt)


def _postmix_kernel(x_ref, mod_ref, attn_ref, sg_ref, gouta_ref, wout_ref, gpost_ref, gpre_ref, wgate_ref,
                    wup_ref, wdown_ref, gpostf_ref, o_ref):
    x = x_ref[0]
    gate1 = mod_ref[0, :, 2 * D_MODEL:3 * D_MODEL]
    shift2 = mod_ref[0, :, 3 * D_MODEL:4 * D_MODEL]
    scale2 = mod_ref[0, :, 4 * D_MODEL:5 * D_MODEL]
    gate2 = mod_ref[0, :, 5 * D_MODEL:6 * D_MODEL]

    a = _rms(attn_ref[0].astype(F32), gouta_ref[...]).astype(BF16)
    merged = jnp.concatenate([a, sg_ref[0]], axis=1)
    m = _dot(merged, wout_ref[...])
    x1 = x + gate1 * _rms(m, gpost_ref[...])

    h = (_rms(x1, gpre_ref[...]) * (1.0 + scale2) + shift2).astype(BF16)
    piece = D_FF // FF_SPLIT
    f = None
    for c in range(FF_SPLIT):
        cols = slice(c * piece, (c + 1) * piece)
        act = (jax.nn.silu(_dot(h, wgate_ref[:, cols])) * _dot(h, wup_ref[:, cols])).astype(BF16)
        part = _dot(act, wdown_ref[cols, :])
        f = part if f is None else f + part
    o_ref[0] = x1 + gate2 * _rms(f, gpostf_ref[...])


def _postmix(x, mod, attn, sg, p):
    b, s, _ = x.shape
    t = _token_tile(s)
    const2 = lambda bi, i: (0, 0)
    tok = lambda bi, i: (bi, i, 0)
    resident = functools.partial(pl.BlockSpec, index_map=const2, pipeline_mode=pl.Buffered(1))
    return pl.pallas_call(
        _postmix_kernel,
        out_shape=jax.ShapeDtypeStruct(x.shape, F32),
        grid=(b, s // t),
        in_specs=[
            pl.BlockSpec((1, t, D_MODEL), tok),
            pl.BlockSpec((1, 1, N_MOD * D_MODEL), lambda bi, i: (bi, 0, 0)),
            pl.BlockSpec((1, t, D_A), tok),
            pl.BlockSpec((1, t, D_B), tok),
            pl.BlockSpec((1, D_A), const2),
            resident((D_A + D_B, D_MODEL)),
            pl.BlockSpec((1, D_MODEL), const2),
            pl.BlockSpec((1, D_MODEL), const2),
            resident((D_MODEL, D_FF)),
            resident((D_MODEL, D_FF)),
            resident((D_FF, D_MODEL)),
            pl.BlockSpec((1, D_MODEL), const2),
        ],
        out_specs=pl.BlockSpec((1, t, D_MODEL), tok),
        compiler_params=pltpu.CompilerParams(
            dimension_semantics=("arbitrary", "arbitrary"), vmem_limit_bytes=VMEM_LIMIT),
        name="postmix",
    )(x, mod, attn, sg, p["g_out_a"], p["w_out"], p["g_post_mix"], p["g_pre_ffn"], p["w_gate"], p["w_up"],
      p["w_down"], p["g_post_ffn"])


def _pad_heads(w, width):
    k = w.shape[0]
    w = w.reshape(k, N_HEADS, width)
    return jnp.pad(w, ((0, 0), (0, 0), (0, SLAB - width))).reshape(k, N_HEADS * SLAB)


def _layer_params(l, w_in, g_q_a, w_q_b, g_kv_a, w_kv_b, g_sgu, w_spatial, b_spatial, g_out_a, g_out_b, w_out,
                  g_pre_mix, g_post_mix, g_pre_ffn, g_post_ffn, w_gate, w_up, w_down):
    o_kr = Q_LORA + KV_LORA
    o_u = o_kr + QK_ROPE
    wi = w_in[l]
    kr_slab = jnp.pad(wi[:, o_kr:o_u], ((0, 0), (QK_NOPE, SLAB - QK_NOPE - QK_ROPE)))
    w_in_p = jnp.concatenate([wi[:, :o_kr], wi[:, o_u:], kr_slab], axis=1)
    kvw = w_kv_b[l].reshape(KV_LORA, N_HEADS, QK_NOPE + V_DIM)
    w_k = _pad_heads(kvw[:, :, :QK_NOPE].reshape(KV_LORA, N_HEADS * QK_NOPE), QK_NOPE)
    w_v = _pad_heads(kvw[:, :, QK_NOPE:].reshape(KV_LORA, N_HEADS * V_DIM), V_DIM)
    ws = w_spatial[l]
    ws_pairs = jnp.concatenate([ws[0::2], ws[1::2]], axis=2)
    bias = jnp.repeat(b_spatial[l].T, C_B, axis=1)
    row = lambda g: g[l].reshape(1, -1)
    return dict(
        g_pre_mix=row(g_pre_mix), w_in=w_in_p.astype(BF16), g_q_a=row(g_q_a),
        w_qt=_pad_heads(w_q_b[l], QK_NOPE + QK_ROPE).T.astype(BF16), g_kv_a=row(g_kv_a),
        w_k=w_k.astype(BF16), w_vt=w_v.T.astype(BF16), g_sgu=row(g_sgu),
        w_spatial=ws_pairs.astype(BF16), b_spatial=bias, g_out_b=row(g_out_b), g_out_a=row(g_out_a),
        w_out=w_out[l].astype(BF16), g_post_mix=row(g_post_mix), g_pre_ffn=row(g_pre_ffn),
        w_gate=w_gate[l].astype(BF16), w_up=w_up[l].astype(BF16), w_down=w_down[l].astype(BF16),
        g_post_ffn=row(g_post_ffn))


def _rope_tables(seq):
    pos = jnp.arange(seq, dtype=F32)
    inv_freq = ROPE_THETA ** (-jnp.arange(0, QK_ROPE, 2, dtype=F32) / QK_ROPE)
    ang = pos[:, None] * inv_freq[None, :]
    cos, sin = jnp.cos(ang), jnp.sin(ang)
    zeros = lambda n: jnp.zeros((seq, n), F32)
    tail = SLAB - QK_NOPE - QK_ROPE
    c = jnp.concatenate([jnp.ones((seq, QK_NOPE), F32), cos, cos, zeros(tail)], axis=1)
    s_lo = jnp.concatenate([zeros(QK_NOPE), -sin, zeros(ROPE_HALF + tail)], axis=1)
    s_hi = jnp.concatenate([zeros(QK_NOPE + ROPE_HALF), sin, zeros(tail)], axis=1)
    return jnp.stack([c, s_lo, s_hi]), jnp.stack([cos.T, sin.T])


def _trunk(x, mod, layers):
    rope_k, rope_q = _rope_tables(x.shape[1])
    for l in range(DEPTH):
        mod_l = mod[l][:, None, :]
        qt, k, vt, sg = _premix(x, mod_l, layers[l], rope_k, rope_q)
        attn = _attention(qt, k, vt)
        x = _postmix(x, mod_l, attn, sg, layers[l])
    return x


def kernel(x_prompt, x_sample, c_prompt, c_sample, w_mod, b_mod, g_pre_mix, g_post_mix, g_pre_ffn, g_post_ffn,
           w_in, g_q_a, w_q_b, g_kv_a, w_kv_b, g_sgu, w_spatial, b_spatial, g_out_a, g_out_b, w_out,
           w_gate, w_up, w_down):
    layers = [
        _layer_params(l, w_in, g_q_a, w_q_b, g_kv_a, w_kv_b, g_sgu, w_spatial, b_spatial, g_out_a, g_out_b, w_out,
                      g_pre_mix, g_post_mix, g_pre_ffn, g_post_ffn, w_gate, w_up, w_down)
        for l in range(DEPTH)
    ]
    nb_p, nb_s = c_prompt.shape[0], c_sample.shape[0]
    rows = -(-(nb_p + nb_s) // 8) * 8
    c_all = jnp.concatenate([c_prompt, c_sample, jnp.zeros((rows - nb_p - nb_s, D_MODEL), F32)], axis=0)
    mod = _modulation(c_all, w_mod, b_mod)
    y_prompt = _trunk(x_prompt, mod[:, :nb_p], layers)
    y_sample = _trunk(x_sample, mod[:, nb_p:nb_p + nb_s], layers)
    return (y_prompt, y_sample)
```

```python
import functools
import math

import jax
import jax.numpy as jnp
from jax import lax
from jax.experimental import pallas as pl
from jax.experimental.pallas import tpu as pltpu

D_MODEL = 1024
DEPTH = 2
N_HEADS = 8
QK_NOPE = 64
QK_ROPE = 32
ROPE_HALF = QK_ROPE // 2
V_DIM = 64
Q_LORA = 384
KV_LORA = 256
D_A = N_HEADS * V_DIM
D_B = 512
G_B = 8
C_B = D_B // G_B
CHUNK = 128
D_FF = 2816
EPS = 1e-6
N_MOD = 6
ROPE_THETA = 10000.0

LANES = 128
SLAB = LANES
V_SLAB = 80
HEADS_PER_STEP = 2
Q_TILE = 512
KV_CHUNK = 2048
INIT_KEYS = 128
STAB_ROW = QK_NOPE + QK_ROPE
STAB_ROWS = 16
GROWTH_LIMIT = 64.0
IN_COLS_PADDED = Q_LORA + KV_LORA + 2 * D_B + SLAB
MXU_DEPTH = 256
FF_PIECES = (0, 6 * MXU_DEPTH, D_FF)
VMEM_LIMIT = 56 * 1024 * 1024

BF16 = jnp.bfloat16
F32 = jnp.float32
NEG_BIG = -1e30
Q_SCALE = math.log2(math.e) / math.sqrt(QK_NOPE + QK_ROPE)


def _token_tile(seq):
    return min(seq, 512)


def _rms(x, g):
    r = lax.rsqrt(jnp.mean(x * x, axis=-1, keepdims=True) + EPS)
    return x * r * g


def _dot(a, b):
    return jnp.dot(a, b, preferred_element_type=F32)


def _dot_nt(a, b):
    return lax.dot_general(a, b, (((1,), (1,)), ((), ())), preferred_element_type=F32)


def _mod_kernel(c_ref, w_ref, b_ref, o_ref):
    cs = jax.nn.silu(c_ref[...])
    o_ref[0] = _dot(cs.astype(BF16), w_ref[0].astype(BF16)) + b_ref[0]


def _modulation(c_all, w_mod, b_mod):
    rows = c_all.shape[0]
    n = w_mod.shape[-1]
    tn = 1536
    return pl.pallas_call(
        _mod_kernel,
        out_shape=jax.ShapeDtypeStruct((DEPTH, rows, n), F32),
        grid=(DEPTH, n // tn),
        in_specs=[
            pl.BlockSpec((rows, D_MODEL), lambda l, j: (0, 0)),
            pl.BlockSpec((1, D_MODEL, tn), lambda l, j: (l, 0, j)),
            pl.BlockSpec((1, 1, tn), lambda l, j: (l, 0, j)),
        ],
        out_specs=pl.BlockSpec((1, rows, tn), lambda l, j: (l, 0, j)),
        compiler_params=pltpu.CompilerParams(
            dimension_semantics=("arbitrary", "arbitrary"), vmem_limit_bytes=VMEM_LIMIT),
        name="adaln_modulation",
    )(c_all, w_mod, b_mod.reshape(DEPTH, 1, n))


def _rope_slab(x, c, s_lo, s_hi):
    return x * c + pltpu.roll(x, SLAB - ROPE_HALF, 1) * s_lo + pltpu.roll(x, ROPE_HALF, 1) * s_hi


def _premix_kernel(x_ref, mod_ref, gpre_ref, win_ref, gq_ref, wqt_ref, gkv_ref, wk_ref, wvt_ref, gsgu_ref, ws_ref,
                   bs_ref, goutb_ref, ropek_ref, ropeq_ref, qt_out, k_out, vt_out, sg_out, mixed_sc):
    x = x_ref[0]
    shift = mod_ref[0, :, 0:D_MODEL]
    scale = mod_ref[0, :, D_MODEL:2 * D_MODEL]
    h = _rms(x, gpre_ref[...]) * (1.0 + scale) + shift
    z = _dot(h.astype(BF16), win_ref[...])

    o_kv, o_u, o_v, o_kr = Q_LORA, Q_LORA + KV_LORA, Q_LORA + KV_LORA + D_B, Q_LORA + KV_LORA + 2 * D_B

    qn = _rms(z[:, 0:Q_LORA], gq_ref[...]).astype(BF16)
    qt = _dot_nt(wqt_ref[...], qn)
    cos_t, sin_t = ropeq_ref[0], ropeq_ref[1]
    lo0, hi0, end = QK_NOPE, QK_NOPE + ROPE_HALF, QK_NOPE + QK_ROPE
    for hd in range(N_HEADS):
        base = hd * SLAB
        x_lo = qt[base + lo0:base + hi0]
        x_hi = qt[base + hi0:base + end]
        slab = jnp.concatenate([
            qt[base:base + lo0] * Q_SCALE,
            (x_lo * cos_t - x_hi * sin_t) * Q_SCALE,
            (x_lo * sin_t + x_hi * cos_t) * Q_SCALE,
            qt[base + end:base + SLAB],
        ], axis=0)
        qt_out[0, base:base + SLAB, :] = slab.astype(BF16)

    kvn = _rms(z[:, o_kv:o_u], gkv_ref[...]).astype(BF16)
    kk = _dot(kvn, wk_ref[...])
    kr = _rope_slab(z[:, o_kr:o_kr + SLAB], ropek_ref[0], ropek_ref[1], ropek_ref[2])
    kr = kr + (lax.broadcasted_iota(jnp.int32, (1, SLAB), 1) == STAB_ROW).astype(F32)
    for hd in range(N_HEADS):
        sl = slice(hd * SLAB, (hd + 1) * SLAB)
        k_out[0, :, sl] = (kk[:, sl] + kr).astype(BF16)
    vt = _dot_nt(wvt_ref[...], kvn)
    ones_row = (lax.broadcasted_iota(jnp.int32, (V_SLAB, 1), 0) == V_DIM).astype(F32)
    for hd in range(N_HEADS):
        base = hd * V_SLAB
        vt_out[0, base:base + V_SLAB, :] = (vt[base:base + V_SLAB] + ones_row).astype(BF16)

    gu = jax.nn.gelu(z[:, o_u:o_v])
    vn = _rms(jax.nn.gelu(z[:, o_v:o_kr]), gsgu_ref[...])
    low = lax.broadcasted_iota(jnp.int32, (CHUNK, SLAB), 1) < C_B
    n_chunks = x.shape[0] // CHUNK
    for c in range(n_chunks):
        rows = slice(c * CHUNK, (c + 1) * CHUNK)
        for j in range(D_B // SLAB):
            cols = slice(j * SLAB, (j + 1) * SLAB)
            slab = vn[rows, cols]
            rhs = jnp.concatenate([jnp.where(low, slab, 0.0), jnp.where(low, 0.0, slab)], axis=0).astype(BF16)
            mixed_sc[rows, cols] = _dot(ws_ref[j], rhs) + bs_ref[:, cols]
    sgu = gu * mixed_sc[...]
    sg_out[0] = _rms(sgu, goutb_ref[...]).astype(BF16)


def _premix(x, mod, p, rope_k, rope_q):
    b, s, _ = x.shape
    t = _token_tile(s)
    const2 = lambda bi, i: (0, 0)
    const3 = lambda bi, i: (0, 0, 0)
    tok = lambda bi, i: (bi, i, 0)
    tok_t = lambda bi, i: (bi, 0, i)
    n_slab = N_HEADS * SLAB
    return pl.pallas_call(
        _premix_kernel,
        out_shape=(
            jax.ShapeDtypeStruct((b, n_slab, s), BF16),
            jax.ShapeDtypeStruct((b, s, n_slab), BF16),
            jax.ShapeDtypeStruct((b, N_HEADS * V_SLAB, s), BF16),
            jax.ShapeDtypeStruct((b, s, D_B), BF16),
        ),
        grid=(b, s // t),
        in_specs=[
            pl.BlockSpec((1, t, D_MODEL), tok),
            pl.BlockSpec((1, 1, N_MOD * D_MODEL), lambda bi, i: (bi, 0, 0)),
            pl.BlockSpec((1, D_MODEL), const2),
            pl.BlockSpec((D_MODEL, IN_COLS_PADDED), const2),
            pl.BlockSpec((1, Q_LORA), const2),
            pl.BlockSpec((n_slab, Q_LORA), const2),
            pl.BlockSpec((1, KV_LORA), const2),
            pl.BlockSpec((KV_LORA, n_slab), const2),
            pl.BlockSpec((N_HEADS * V_SLAB, KV_LORA), const2),
            pl.BlockSpec((1, D_B), const2),
            pl.BlockSpec((D_B // SLAB, CHUNK, 2 * CHUNK), const3),
            pl.BlockSpec((CHUNK, D_B), const2),
            pl.BlockSpec((1, D_B), const2),
            pl.BlockSpec((3, t, SLAB), lambda bi, i: (0, i, 0)),
            pl.BlockSpec((2, ROPE_HALF, t), lambda bi, i: (0, 0, i)),
        ],
        out_specs=(
            pl.BlockSpec((1, n_slab, t), tok_t),
            pl.BlockSpec((1, t, n_slab), tok),
            pl.BlockSpec((1, N_HEADS * V_SLAB, t), tok_t),
            pl.BlockSpec((1, t, D_B), tok),
        ),
        scratch_shapes=[pltpu.VMEM((t, D_B), F32)],
        compiler_params=pltpu.CompilerParams(
            dimension_semantics=("arbitrary", "arbitrary"), vmem_limit_bytes=VMEM_LIMIT),
        name="premix",
    )(x, mod, p["g_pre_mix"], p["w_in"], p["g_q_a"], p["w_qt"], p["g_kv_a"], p["w_k"], p["w_vt"], p["g_sgu"],
      p["w_spatial"], p["b_spatial"], p["g_out_b"], rope_k, rope_q)


def _round_up_bf16(x):
    return (x + jnp.abs(x) * (1.0 / 128.0)).astype(BF16).astype(F32)


def _attn_kernel(qt_ref, k_ref, vt_ref, o_ref, qs_sc, m_sc, acc_sc, *, tk):
    n_chunks = k_ref.shape[1] // tk
    tq = qs_sc.shape[-1]
    heads = range(HEADS_PER_STEP)
    pad_rows = jnp.zeros((STAB_ROWS - 1, tq), BF16)

    def set_stabiliser(hd, m):
        qs_sc[hd, STAB_ROW:STAB_ROW + STAB_ROWS, :] = jnp.concatenate([(-m).astype(BF16), pad_rows], axis=0)

    def chunk(c, src, dst):
        start = pl.multiple_of(c * tk, tk)
        worst = None
        for hd in heads:
            sl = slice(hd * SLAB, (hd + 1) * SLAB)
            m_old = m_sc[src, hd]
            st = _dot(k_ref[0, pl.ds(start, tk), sl], qs_sc[hd])
            mx = jnp.max(st, axis=0, keepdims=True)
            vt = vt_ref[0, hd * V_SLAB:(hd + 1) * V_SLAB, pl.ds(start, tk)]
            pv = _dot(vt, jnp.exp2(st).astype(BF16))
            m_new = jnp.where(mx > 0.0, _round_up_bf16(m_old + mx), m_old)
            acc_sc[dst, hd] = (acc_sc[src, hd] + pv) * jnp.exp2(m_old - m_new)
            m_sc[dst, hd] = m_new
            top = jnp.max(mx)
            worst = top if worst is None else jnp.maximum(worst, top)
        return worst

    def step(c, src, dst):
        worst = chunk(c, src, dst)

        @pl.when(worst > GROWTH_LIMIT)
        def _():
            for hd in heads:
                m_new = m_sc[dst, hd]
                acc_sc[src, hd] = acc_sc[src, hd] * jnp.exp2(m_sc[src, hd] - m_new)
                m_sc[src, hd] = m_new
                set_stabiliser(hd, m_new)
            chunk(c, src, dst)

        for hd in heads:
            set_stabiliser(hd, m_sc[dst, hd])

    for hd in heads:
        sl = slice(hd * SLAB, (hd + 1) * SLAB)
        qs_sc[hd] = qt_ref[0, sl, :]
        m0 = _round_up_bf16(jnp.max(_dot(k_ref[0, 0:INIT_KEYS, sl], qs_sc[hd]), axis=0, keepdims=True))
        m_sc[0, hd] = m0
        acc_sc[0, hd] = jnp.zeros(acc_sc.shape[2:], F32)
        set_stabiliser(hd, m0)

    def body(c, carry):
        for par in range(2):
            @pl.when(c % 2 == par)
            def _():
                step(c, par, 1 - par)
        return carry

    lax.fori_loop(0, n_chunks, body, 0)
    last = n_chunks % 2
    outs = []
    for hd in heads:
        acc = acc_sc[last, hd]
        outs.append(acc[0:V_DIM] / acc[V_DIM:V_DIM + 1])
    o_ref[0] = jnp.concatenate(outs, axis=0).T.astype(o_ref.dtype)


def _attention(qt, k, vt):
    b, s, _ = k.shape
    tq = min(s, Q_TILE)
    tk = min(s, KV_CHUNK)
    assert s % tq == 0 and s % tk == 0 and tk % LANES == 0 and s >= INIT_KEYS
    w = HEADS_PER_STEP * SLAB
    return pl.pallas_call(
        functools.partial(_attn_kernel, tk=tk),
        out_shape=jax.ShapeDtypeStruct((b, s, D_A), BF16),
        grid=(b, N_HEADS // HEADS_PER_STEP, s // tq),
        in_specs=[
            pl.BlockSpec((1, w, tq), lambda bi, hp, i: (bi, hp, i)),
            pl.BlockSpec((1, s, w), lambda bi, hp, i: (bi, 0, hp)),
            pl.BlockSpec((1, HEADS_PER_STEP * V_SLAB, s), lambda bi, hp, i: (bi, hp, 0)),
        ],
        out_specs=pl.BlockSpec((1, tq, HEADS_PER_STEP * V_DIM), lambda bi, hp, i: (bi, i, hp)),
        scratch_shapes=[
            pltpu.VMEM((HEADS_PER_STEP, SLAB, tq), BF16),
            pltpu.VMEM((2, HEADS_PER_STEP, 1, tq), F32),
            pltpu.VMEM((2, HEADS_PER_STEP, V_SLAB, tq), F32),
        ],
        compiler_params=pltpu.CompilerParams(
            dimension_semantics=("arbitrary", "arbitrary", "arbitrary"), vmem_limit_bytes=VMEM_LIMIT,
        ),
        name="mla_attention",
    )(qt, k, vt)


def _postmix_kernel(x_ref, mod_ref, attn_ref, sg_ref, gouta_ref, wout_ref, gpost_ref, gpre_ref, wgate_ref,
                    wup_ref, wdown_ref, gpostf_ref, o_ref):
    x = x_ref[0]
    gate1 = mod_ref[0, :, 2 * D_MODEL:3 * D_MODEL]
    shift2 = mod_ref[0, :, 3 * D_MODEL:4 * D_MODEL]
    scale2 = mod_ref[0, :, 4 * D_MODEL:5 * D_MODEL]
    gate2 = mod_ref[0, :, 5 * D_MODEL:6 * D_MODEL]

    a = _rms(attn_ref[0].astype(F32), gouta_ref[...]).astype(BF16)
    merged = jnp.concatenate([a, sg_ref[0]], axis=1)
    m = _dot(merged, wout_ref[...])
    x1 = x + gate1 * _rms(m, gpost_ref[...])

    h = (_rms(x1, gpre_ref[...]) * (1.0 + scale2) + shift2).astype(BF16)
    f = None
    for lo, hi in zip(FF_PIECES[:-1], FF_PIECES[1:]):
        cols = slice(lo, hi)
        act = (jax.nn.silu(_dot(h, wgate_ref[:, cols])) * _dot(h, wup_ref[:, cols])).astype(BF16)
        part = _dot(act, wdown_ref[cols, :])
        f = part if f is None else f + part
    o_ref[0] = x1 + gate2 * _rms(f, gpostf_ref[...])


def _postmix(x, mod, attn, sg, p):
    b, s, _ = x.shape
    t = _token_tile(s)
    const2 = lambda bi, i: (0, 0)
    tok = lambda bi, i: (bi, i, 0)
    resident = functools.partial(pl.BlockSpec, index_map=const2, pipeline_mode=pl.Buffered(1))
    return pl.pallas_call(
        _postmix_kernel,
        out_shape=jax.ShapeDtypeStruct(x.shape, F32),
        grid=(b, s // t),
        in_specs=[
            pl.BlockSpec((1, t, D_MODEL), tok),
            pl.BlockSpec((1, 1, N_MOD * D_MODEL), lambda bi, i: (bi, 0, 0)),
            pl.BlockSpec((1, t, D_A), tok),
            pl.BlockSpec((1, t, D_B), tok),
            pl.BlockSpec((1, D_A), const2),
            resident((D_A + D_B, D_MODEL)),
            pl.BlockSpec((1, D_MODEL), const2),
            pl.BlockSpec((1, D_MODEL), const2),
            resident((D_MODEL, D_FF)),
            resident((D_MODEL, D_FF)),
            resident((D_FF, D_MODEL)),
            pl.BlockSpec((1, D_MODEL), const2),
        ],
        out_specs=pl.BlockSpec((1, t, D_MODEL), tok),
        compiler_params=pltpu.CompilerParams(
            dimension_semantics=("arbitrary", "arbitrary"), vmem_limit_bytes=VMEM_LIMIT),
        name="postmix",
    )(x, mod, attn, sg, p["g_out_a"], p["w_out"], p["g_post_mix"], p["g_pre_ffn"], p["w_gate"], p["w_up"],
      p["w_down"], p["g_post_ffn"])


def _pad_heads(w, width, slab=SLAB):
    k = w.shape[0]
    w = w.reshape(k, N_HEADS, width)
    return jnp.pad(w, ((0, 0), (0, 0), (0, slab - width))).reshape(k, N_HEADS * slab)


def _layer_params(l, w_in, g_q_a, w_q_b, g_kv_a, w_kv_b, g_sgu, w_spatial, b_spatial, g_out_a, g_out_b, w_out,
                  g_pre_mix, g_post_mix, g_pre_ffn, g_post_ffn, w_gate, w_up, w_down):
    o_kr = Q_LORA + KV_LORA
    o_u = o_kr + QK_ROPE
    wi = w_in[l]
    kr_slab = jnp.pad(wi[:, o_kr:o_u], ((0, 0), (QK_NOPE, SLAB - QK_NOPE - QK_ROPE)))
    w_in_p = jnp.concatenate([wi[:, :o_kr], wi[:, o_u:], kr_slab], axis=1)
    kvw = w_kv_b[l].reshape(KV_LORA, N_HEADS, QK_NOPE + V_DIM)
    w_k = _pad_heads(kvw[:, :, :QK_NOPE].reshape(KV_LORA, N_HEADS * QK_NOPE), QK_NOPE)
    w_v = _pad_heads(kvw[:, :, QK_NOPE:].reshape(KV_LORA, N_HEADS * V_DIM), V_DIM, V_SLAB)
    ws = w_spatial[l]
    ws_pairs = jnp.concatenate([ws[0::2], ws[1::2]], axis=2)
    bias = jnp.repeat(b_spatial[l].T, C_B, axis=1)
    row = lambda g: g[l].reshape(1, -1)
    return dict(
        g_pre_mix=row(g_pre_mix), w_in=w_in_p.astype(BF16), g_q_a=row(g_q_a),
        w_qt=_pad_heads(w_q_b[l], QK_NOPE + QK_ROPE).T.astype(BF16), g_kv_a=row(g_kv_a),
        w_k=w_k.astype(BF16), w_vt=w_v.T.astype(BF16), g_sgu=row(g_sgu),
        w_spatial=ws_pairs.astype(BF16), b_spatial=bias, g_out_b=row(g_out_b), g_out_a=row(g_out_a),
        w_out=w_out[l].astype(BF16), g_post_mix=row(g_post_mix), g_pre_ffn=row(g_pre_ffn),
        w_gate=w_gate[l].astype(BF16), w_up=w_up[l].astype(BF16), w_down=w_down[l].astype(BF16),
        g_post_ffn=row(g_post_ffn))


def _rope_tables(seq):
    pos = jnp.arange(seq, dtype=F32)
    inv_freq = ROPE_THETA ** (-jnp.arange(0, QK_ROPE, 2, dtype=F32) / QK_ROPE)
    ang = pos[:, None] * inv_freq[None, :]
    cos, sin = jnp.cos(ang), jnp.sin(ang)
    zeros = lambda n: jnp.zeros((seq, n), F32)
    tail = SLAB - QK_NOPE - QK_ROPE
    c = jnp.concatenate([jnp.ones((seq, QK_NOPE), F32), cos, cos, zeros(tail)], axis=1)
    s_lo = jnp.concatenate([zeros(QK_NOPE), -sin, zeros(ROPE_HALF + tail)], axis=1)
    s_hi = jnp.concatenate([zeros(QK_NOPE + ROPE_HALF), sin, zeros(tail)], axis=1)
    return jnp.stack([c, s_lo, s_hi]), jnp.stack([cos.T, sin.T])


def _trunk(x, mod, layers):
    rope_k, rope_q = _rope_tables(x.shape[1])
    for l in range(DEPTH):
        mod_l = mod[l][:, None, :]
        qt, k, vt, sg = _premix(x, mod_l, layers[l], rope_k, rope_q)
        attn = _attention(qt, k, vt)
        x = _postmix(x, mod_l, attn, sg, layers[l])
    return x


def kernel(x_prompt, x_sample, c_prompt, c_sample, w_mod, b_mod, g_pre_mix, g_post_mix, g_pre_ffn, g_post_ffn,
           w_in, g_q_a, w_q_b, g_kv_a, w_kv_b, g_sgu, w_spatial, b_spatial, g_out_a, g_out_b, w_out,
           w_gate, w_up, w_down):
    layers = [
        _layer_params(l, w_in, g_q_a, w_q_b, g_kv_a, w_kv_b, g_sgu, w_spatial, b_spatial, g_out_a, g_out_b, w_out,
                      g_pre_mix, g_post_mix, g_pre_ffn, g_post_ffn, w_gate, w_up, w_down)
        for l in range(DEPTH)
    ]
    nb_p, nb_s = c_prompt.shape[0], c_sample.shape[0]
    rows = -(-(nb_p + nb_s) // 8) * 8
    c_all = jnp.concatenate([c_prompt, c_sample, jnp.zeros((rows - nb_p - nb_s, D_MODEL), F32)], axis=0)
    mod = _modulation(c_all, w_mod, b_mod)
    y_prompt = _trunk(x_prompt, mod[:, :nb_p], layers)
    y_sample = _trunk(x_sample, mod[:, nb_p:nb_p + nb_s], layers)
    return (y_prompt, y_sample)
```

```python
import functools
import math

import jax
import jax.numpy as jnp
from jax import lax
from jax.experimental import pallas as pl
from jax.experimental.pallas import tpu as pltpu

D_MODEL = 1024
DEPTH = 2
N_HEADS = 8
QK_NOPE = 64
QK_ROPE = 32
ROPE_HALF = QK_ROPE // 2
V_DIM = 64
Q_LORA = 384
KV_LORA = 256
D_A = N_HEADS * V_DIM
D_B = 512
G_B = 8
C_B = D_B // G_B
CHUNK = 128
D_FF = 2816
EPS = 1e-6
N_MOD = 6
ROPE_THETA = 10000.0

LANES = 128
SLAB = LANES
V_SLAB = 80
HEADS_PER_STEP = 4
Q_TILE = 512
KV_CHUNK = 2048
INIT_KEYS = 128
STAB_ROW = QK_NOPE + QK_ROPE
STAB_ROWS = 16
GROWTH_LIMIT = 64.0
IN_COLS_PADDED = Q_LORA + KV_LORA + 2 * D_B + SLAB
TOKEN_SUBTILES = 2
MXU_DEPTH = 256
FF_PIECES = (0, 6 * MXU_DEPTH, D_FF)
VMEM_LIMIT = 56 * 1024 * 1024

BF16 = jnp.bfloat16
F32 = jnp.float32
NEG_BIG = -1e30
Q_SCALE = math.log2(math.e) / math.sqrt(QK_NOPE + QK_ROPE)


def _token_tile(seq):
    return min(seq, 512)


def _rms(x, g):
    r = lax.rsqrt(jnp.mean(x * x, axis=-1, keepdims=True) + EPS)
    return x * r * g


def _dot(a, b):
    return jnp.dot(a, b, preferred_element_type=F32)


def _dot_nt(a, b):
    return lax.dot_general(a, b, (((1,), (1,)), ((), ())), preferred_element_type=F32)


def _mod_kernel(c_ref, w_ref, b_ref, o_ref):
    cs = jax.nn.silu(c_ref[...])
    o_ref[0] = _dot(cs.astype(BF16), w_ref[0].astype(BF16)) + b_ref[0]


def _modulation(c_all, w_mod, b_mod):
    rows = c_all.shape[0]
    n = w_mod.shape[-1]
    tn = 1536
    return pl.pallas_call(
        _mod_kernel,
        out_shape=jax.ShapeDtypeStruct((DEPTH, rows, n), F32),
        grid=(DEPTH, n // tn),
        in_specs=[
            pl.BlockSpec((rows, D_MODEL), lambda l, j: (0, 0)),
            pl.BlockSpec((1, D_MODEL, tn), lambda l, j: (l, 0, j)),
            pl.BlockSpec((1, 1, tn), lambda l, j: (l, 0, j)),
        ],
        out_specs=pl.BlockSpec((1, rows, tn), lambda l, j: (l, 0, j)),
        compiler_params=pltpu.CompilerParams(
            dimension_semantics=("arbitrary", "arbitrary"), vmem_limit_bytes=VMEM_LIMIT),
        name="adaln_modulation",
    )(c_all, w_mod, b_mod.reshape(DEPTH, 1, n))


def _rope_slab(x, c, s_lo, s_hi):
    return x * c + pltpu.roll(x, SLAB - ROPE_HALF, 1) * s_lo + pltpu.roll(x, ROPE_HALF, 1) * s_hi


def _premix_kernel(x_ref, mod_ref, gpre_ref, win_ref, gq_ref, wqt_ref, gkv_ref, wk_ref, wvt_ref, gsgu_ref, ws_ref,
                   bs_ref, goutb_ref, ropek_ref, ropeq_ref, qt_out, k_out, vt_out, sg_out, mixed_sc):
    shift = mod_ref[0, :, 0:D_MODEL]
    scale = mod_ref[0, :, D_MODEL:2 * D_MODEL]
    t = x_ref.shape[1]
    n_sub = TOKEN_SUBTILES if t % (TOKEN_SUBTILES * CHUNK) == 0 else 1
    blocks = [slice(i * t // n_sub, (i + 1) * t // n_sub) for i in range(n_sub)]
    o_kv, o_u, o_v, o_kr = Q_LORA, Q_LORA + KV_LORA, Q_LORA + KV_LORA + D_B, Q_LORA + KV_LORA + 2 * D_B

    z = []
    for r in blocks:
        h = _rms(x_ref[0, r], gpre_ref[...]) * (1.0 + scale) + shift
        z.append(_dot(h.astype(BF16), win_ref[...]))

    lo0, hi0, end = QK_NOPE, QK_NOPE + ROPE_HALF, QK_NOPE + QK_ROPE
    qt = [_dot_nt(wqt_ref[...], _rms(zi[:, 0:Q_LORA], gq_ref[...]).astype(BF16)) for zi in z]
    kvn = [_rms(zi[:, o_kv:o_u], gkv_ref[...]).astype(BF16) for zi in z]
    kk = [_dot(v, wk_ref[...]) for v in kvn]
    vt = [_dot_nt(wvt_ref[...], v) for v in kvn]
    ones_lane = (lax.broadcasted_iota(jnp.int32, (1, SLAB), 1) == STAB_ROW).astype(F32)
    ones_row = (lax.broadcasted_iota(jnp.int32, (V_SLAB, 1), 0) == V_DIM).astype(F32)
    for r, zi, qti, kki, vti in zip(blocks, z, qt, kk, vt):
        cos_t, sin_t = ropeq_ref[0, :, r], ropeq_ref[1, :, r]
        for hd in range(N_HEADS):
            base = hd * SLAB
            x_lo = qti[base + lo0:base + hi0]
            x_hi = qti[base + hi0:base + end]
            slab = jnp.concatenate([
                qti[base:base + lo0] * Q_SCALE,
                (x_lo * cos_t - x_hi * sin_t) * Q_SCALE,
                (x_lo * sin_t + x_hi * cos_t) * Q_SCALE,
                qti[base + end:base + SLAB],
            ], axis=0)
            qt_out[0, base:base + SLAB, r] = slab.astype(BF16)
        kr = _rope_slab(zi[:, o_kr:o_kr + SLAB], ropek_ref[0, r], ropek_ref[1, r], ropek_ref[2, r]) + ones_lane
        for hd in range(N_HEADS):
            sl = slice(hd * SLAB, (hd + 1) * SLAB)
            k_out[0, r, sl] = (kki[:, sl] + kr).astype(BF16)
        for hd in range(N_HEADS):
            base = hd * V_SLAB
            vt_out[0, base:base + V_SLAB, r] = (vti[base:base + V_SLAB] + ones_row).astype(BF16)

    low = lax.broadcasted_iota(jnp.int32, (CHUNK, SLAB), 1) < C_B
    for r, zi in zip(blocks, z):
        gu = jax.nn.gelu(zi[:, o_u:o_v])
        vn = _rms(jax.nn.gelu(zi[:, o_v:o_kr]), gsgu_ref[...])
        for c in range((r.stop - r.start) // CHUNK):
            rows = slice(c * CHUNK, (c + 1) * CHUNK)
            out_rows = slice(r.start + c * CHUNK, r.start + (c + 1) * CHUNK)
            for j in range(D_B // SLAB):
                cols = slice(j * SLAB, (j + 1) * SLAB)
                slab = vn[rows, cols]
                rhs = jnp.concatenate([jnp.where(low, slab, 0.0), jnp.where(low, 0.0, slab)], axis=0).astype(BF16)
                mixed_sc[out_rows, cols] = _dot(ws_ref[j], rhs) + bs_ref[:, cols]
        sgu = gu * mixed_sc[r]
        sg_out[0, r] = _rms(sgu, goutb_ref[...]).astype(BF16)


def _premix(x, mod, p, rope_k, rope_q):
    b, s, _ = x.shape
    t = _token_tile(s)
    const2 = lambda bi, i: (0, 0)
    const3 = lambda bi, i: (0, 0, 0)
    tok = lambda bi, i: (bi, i, 0)
    tok_t = lambda bi, i: (bi, 0, i)
    n_slab = N_HEADS * SLAB
    return pl.pallas_call(
        _premix_kernel,
        out_shape=(
            jax.ShapeDtypeStruct((b, n_slab, s), BF16),
            jax.ShapeDtypeStruct((b, s, n_slab), BF16),
            jax.ShapeDtypeStruct((b, N_HEADS * V_SLAB, s), BF16),
            jax.ShapeDtypeStruct((b, s, D_B), BF16),
        ),
        grid=(b, s // t),
        in_specs=[
            pl.BlockSpec((1, t, D_MODEL), tok),
            pl.BlockSpec((1, 1, N_MOD * D_MODEL), lambda bi, i: (bi, 0, 0)),
            pl.BlockSpec((1, D_MODEL), const2),
            pl.BlockSpec((D_MODEL, IN_COLS_PADDED), const2),
            pl.BlockSpec((1, Q_LORA), const2),
            pl.BlockSpec((n_slab, Q_LORA), const2),
            pl.BlockSpec((1, KV_LORA), const2),
            pl.BlockSpec((KV_LORA, n_slab), const2),
            pl.BlockSpec((N_HEADS * V_SLAB, KV_LORA), const2),
            pl.BlockSpec((1, D_B), const2),
            pl.BlockSpec((D_B // SLAB, CHUNK, 2 * CHUNK), const3),
            pl.BlockSpec((CHUNK, D_B), const2),
            pl.BlockSpec((1, D_B), const2),
            pl.BlockSpec((3, t, SLAB), lambda bi, i: (0, i, 0)),
            pl.BlockSpec((2, ROPE_HALF, t), lambda bi, i: (0, 0, i)),
        ],
        out_specs=(
            pl.BlockSpec((1, n_slab, t), tok_t),
            pl.BlockSpec((1, t, n_slab), tok),
            pl.BlockSpec((1, N_HEADS * V_SLAB, t), tok_t),
            pl.BlockSpec((1, t, D_B), tok),
        ),
        scratch_shapes=[pltpu.VMEM((t, D_B), F32)],
        compiler_params=pltpu.CompilerParams(
            dimension_semantics=("arbitrary", "arbitrary"), vmem_limit_bytes=VMEM_LIMIT),
        name="premix",
    )(x, mod, p["g_pre_mix"], p["w_in"], p["g_q_a"], p["w_qt"], p["g_kv_a"], p["w_k"], p["w_vt"], p["g_sgu"],
      p["w_spatial"], p["b_spatial"], p["g_out_b"], rope_k, rope_q)


def _round_up_bf16(x):
    return (x + jnp.abs(x) * (1.0 / 128.0)).astype(BF16).astype(F32)


def _attn_kernel(qt_ref, k_ref, vt_ref, o_ref, qs_sc, m_sc, acc_sc, *, tk):
    n_chunks = k_ref.shape[1] // tk
    tq = qs_sc.shape[-1]
    heads = range(HEADS_PER_STEP)
    pad_rows = jnp.zeros((STAB_ROWS - 1, tq), BF16)

    def set_stabiliser(hd, m):
        qs_sc[hd, STAB_ROW:STAB_ROW + STAB_ROWS, :] = jnp.concatenate([(-m).astype(BF16), pad_rows], axis=0)

    def chunk(c, src, dst):
        start = pl.multiple_of(c * tk, tk)
        worst = None
        for hd in heads:
            sl = slice(hd * SLAB, (hd + 1) * SLAB)
            m_old = m_sc[src, hd]
            st = _dot(k_ref[0, pl.ds(start, tk), sl], qs_sc[hd])
            mx = jnp.max(st, axis=0, keepdims=True)
            vt = vt_ref[0, hd * V_SLAB:(hd + 1) * V_SLAB, pl.ds(start, tk)]
            pv = _dot(vt, jnp.exp2(st).astype(BF16))
            m_new = jnp.where(mx > 0.0, _round_up_bf16(m_old + mx), m_old)
            acc_sc[dst, hd] = (acc_sc[src, hd] + pv) * jnp.exp2(m_old - m_new)
            m_sc[dst, hd] = m_new
            top = jnp.max(mx)
            worst = top if worst is None else jnp.maximum(worst, top)
        return worst

    def step(c, src, dst):
        worst = chunk(c, src, dst)

        @pl.when(worst > GROWTH_LIMIT)
        def _():
            for hd in heads:
                m_new = m_sc[dst, hd]
                acc_sc[src, hd] = acc_sc[src, hd] * jnp.exp2(m_sc[src, hd] - m_new)
                m_sc[src, hd] = m_new
                set_stabiliser(hd, m_new)
            chunk(c, src, dst)

        for hd in heads:
            set_stabiliser(hd, m_sc[dst, hd])

    for hd in heads:
        sl = slice(hd * SLAB, (hd + 1) * SLAB)
        qs_sc[hd] = qt_ref[0, sl, :]
        m0 = _round_up_bf16(jnp.max(_dot(k_ref[0, 0:INIT_KEYS, sl], qs_sc[hd]), axis=0, keepdims=True))
        m_sc[0, hd] = m0
        acc_sc[0, hd] = jnp.zeros(acc_sc.shape[2:], F32)
        set_stabiliser(hd, m0)

    def body(c, carry):
        for par in range(2):
            @pl.when(c % 2 == par)
            def _():
                step(c, par, 1 - par)
        return carry

    lax.fori_loop(0, n_chunks, body, 0)
    last = n_chunks % 2
    outs = []
    for hd in heads:
        acc = acc_sc[last, hd]
        outs.append(acc[0:V_DIM] / acc[V_DIM:V_DIM + 1])
    o_ref[0] = jnp.concatenate(outs, axis=0).T.astype(o_ref.dtype)


def _attention(qt, k, vt):
    b, s, _ = k.shape
    tq = min(s, Q_TILE)
    tk = min(s, KV_CHUNK)
    assert s % tq == 0 and s % tk == 0 and tk % LANES == 0 and s >= INIT_KEYS
    w = HEADS_PER_STEP * SLAB
    return pl.pallas_call(
        functools.partial(_attn_kernel, tk=tk),
        out_shape=jax.ShapeDtypeStruct((b, s, D_A), BF16),
        grid=(b, N_HEADS // HEADS_PER_STEP, s // tq),
        in_specs=[
            pl.BlockSpec((1, w, tq), lambda bi, hp, i: (bi, hp, i)),
            pl.BlockSpec((1, s, w), lambda bi, hp, i: (bi, 0, hp)),
            pl.BlockSpec((1, HEADS_PER_STEP * V_SLAB, s), lambda bi, hp, i: (bi, hp, 0)),
        ],
        out_specs=pl.BlockSpec((1, tq, HEADS_PER_STEP * V_DIM), lambda bi, hp, i: (bi, i, hp)),
        scratch_shapes=[
            pltpu.VMEM((HEADS_PER_STEP, SLAB, tq), BF16),
            pltpu.VMEM((2, HEADS_PER_STEP, 1, tq), F32),
            pltpu.VMEM((2, HEADS_PER_STEP, V_SLAB, tq), F32),
        ],
        compiler_params=pltpu.CompilerParams(
            dimension_semantics=("arbitrary", "arbitrary", "arbitrary"), vmem_limit_bytes=VMEM_LIMIT,
        ),
        name="mla_attention",
    )(qt, k, vt)


def _postmix_kernel(x_ref, mod_ref, attn_ref, sg_ref, gouta_ref, wout_ref, gpost_ref, gpre_ref, wgate_ref,
                    wup_ref, wdown_ref, gpostf_ref, o_ref):
    gate1 = mod_ref[0, :, 2 * D_MODEL:3 * D_MODEL]
    shift2 = mod_ref[0, :, 3 * D_MODEL:4 * D_MODEL]
    scale2 = mod_ref[0, :, 4 * D_MODEL:5 * D_MODEL]
    gate2 = mod_ref[0, :, 5 * D_MODEL:6 * D_MODEL]
    t = x_ref.shape[1]
    n_sub = TOKEN_SUBTILES if t % (TOKEN_SUBTILES * CHUNK) == 0 else 1
    blocks = [slice(i * t // n_sub, (i + 1) * t // n_sub) for i in range(n_sub)]
    merged = [jnp.concatenate([_rms(attn_ref[0, r].astype(F32), gouta_ref[...]).astype(BF16), sg_ref[0, r]], axis=1)
              for r in blocks]
    m = [_dot(v, wout_ref[...]) for v in merged]
    x1 = [x_ref[0, r] + gate1 * _rms(v, gpost_ref[...]) for r, v in zip(blocks, m)]
    h = [(_rms(v, gpre_ref[...]) * (1.0 + scale2) + shift2).astype(BF16) for v in x1]
    f = [None] * n_sub
    for lo, hi in zip(FF_PIECES[:-1], FF_PIECES[1:]):
        cols = slice(lo, hi)
        for i in range(n_sub):
            act = (jax.nn.silu(_dot(h[i], wgate_ref[:, cols])) * _dot(h[i], wup_ref[:, cols])).astype(BF16)
            part = _dot(act, wdown_ref[cols, :])
            f[i] = part if f[i] is None else f[i] + part
    for r, v1, vf in zip(blocks, x1, f):
        o_ref[0, r] = v1 + gate2 * _rms(vf, gpostf_ref[...])


def _postmix(x, mod, attn, sg, p):
    b, s, _ = x.shape
    t = _token_tile(s)
    const2 = lambda bi, i: (0, 0)
    tok = lambda bi, i: (bi, i, 0)
    resident = functools.partial(pl.BlockSpec, index_map=const2, pipeline_mode=pl.Buffered(1))
    return pl.pallas_call(
        _postmix_kernel,
        out_shape=jax.ShapeDtypeStruct(x.shape, F32),
        grid=(b, s // t),
        in_specs=[
            pl.BlockSpec((1, t, D_MODEL), tok),
            pl.BlockSpec((1, 1, N_MOD * D_MODEL), lambda bi, i: (bi, 0, 0)),
            pl.BlockSpec((1, t, D_A), tok),
            pl.BlockSpec((1, t, D_B), tok),
            pl.BlockSpec((1, D_A), const2),
            resident((D_A + D_B, D_MODEL)),
            pl.BlockSpec((1, D_MODEL), const2),
            pl.BlockSpec((1, D_MODEL), const2),
            resident((D_MODEL, D_FF)),
            resident((D_MODEL, D_FF)),
            resident((D_FF, D_MODEL)),
            pl.BlockSpec((1, D_MODEL), const2),
        ],
        out_specs=pl.BlockSpec((1, t, D_MODEL), tok),
        compiler_params=pltpu.CompilerParams(
            dimension_semantics=("arbitrary", "arbitrary"), vmem_limit_bytes=VMEM_LIMIT),
        name="postmix",
    )(x, mod, attn, sg, p["g_out_a"], p["w_out"], p["g_post_mix"], p["g_pre_ffn"], p["w_gate"], p["w_up"],
      p["w_down"], p["g_post_ffn"])


def _pad_heads(w, width, slab=SLAB):
    k = w.shape[0]
    w = w.reshape(k, N_HEADS, width)
    return jnp.pad(w, ((0, 0), (0, 0), (0, slab - width))).reshape(k, N_HEADS * slab)


def _layer_params(l, w_in, g_q_a, w_q_b, g_kv_a, w_kv_b, g_sgu, w_spatial, b_spatial, g_out_a, g_out_b, w_out,
                  g_pre_mix, g_post_mix, g_pre_ffn, g_post_ffn, w_gate, w_up, w_down):
    o_kr = Q_LORA + KV_LORA
    o_u = o_kr + QK_ROPE
    wi = w_in[l]
    kr_slab = jnp.pad(wi[:, o_kr:o_u], ((0, 0), (QK_NOPE, SLAB - QK_NOPE - QK_ROPE)))
    w_in_p = jnp.concatenate([wi[:, :o_kr], wi[:, o_u:], kr_slab], axis=1)
    kvw = w_kv_b[l].reshape(KV_LORA, N_HEADS, QK_NOPE + V_DIM)
    w_k = _pad_heads(kvw[:, :, :QK_NOPE].reshape(KV_LORA, N_HEADS * QK_NOPE), QK_NOPE)
    w_v = _pad_heads(kvw[:, :, QK_NOPE:].reshape(KV_LORA, N_HEADS * V_DIM), V_DIM, V_SLAB)
    ws = w_spatial[l]
    ws_pairs = jnp.concatenate([ws[0::2], ws[1::2]], axis=2)
    bias = jnp.repeat(b_spatial[l].T, C_B, axis=1)
    row = lambda g: g[l].reshape(1, -1)
    return dict(
        g_pre_mix=row(g_pre_mix), w_in=w_in_p.astype(BF16), g_q_a=row(g_q_a),
        w_qt=_pad_heads(w_q_b[l], QK_NOPE + QK_ROPE).T.astype(BF16), g_kv_a=row(g_kv_a),
        w_k=w_k.astype(BF16), w_vt=w_v.T.astype(BF16), g_sgu=row(g_sgu),
        w_spatial=ws_pairs.astype(BF16), b_spatial=bias, g_out_b=row(g_out_b), g_out_a=row(g_out_a),
        w_out=w_out[l].astype(BF16), g_post_mix=row(g_post_mix), g_pre_ffn=row(g_pre_ffn),
        w_gate=w_gate[l].astype(BF16), w_up=w_up[l].astype(BF16), w_down=w_down[l].astype(BF16),
        g_post_ffn=row(g_post_ffn))


def _rope_tables(seq):
    pos = jnp.arange(seq, dtype=F32)
    inv_freq = ROPE_THETA ** (-jnp.arange(0, QK_ROPE, 2, dtype=F32) / QK_ROPE)
    ang = pos[:, None] * inv_freq[None, :]
    cos, sin = jnp.cos(ang), jnp.sin(ang)
    zeros = lambda n: jnp.zeros((seq, n), F32)
    tail = SLAB - QK_NOPE - QK_ROPE
    c = jnp.concatenate([jnp.ones((seq, QK_NOPE), F32), cos, cos, zeros(tail)], axis=1)
    s_lo = jnp.concatenate([zeros(QK_NOPE), -sin, zeros(ROPE_HALF + tail)], axis=1)
    s_hi = jnp.concatenate([zeros(QK_NOPE + ROPE_HALF), sin, zeros(tail)], axis=1)
    return jnp.stack([c, s_lo, s_hi]), jnp.stack([cos.T, sin.T])


def _trunk(x, mod, layers):
    rope_k, rope_q = _rope_tables(x.shape[1])
    for l in range(DEPTH):
        mod_l = mod[l][:, None, :]
        qt, k, vt, sg = _premix(x, mod_l, layers[l], rope_k, rope_q)
        attn = _attention(qt, k, vt)
        x = _postmix(x, mod_l, attn, sg, layers[l])
    return x


def kernel(x_prompt, x_sample, c_prompt, c_sample, w_mod, b_mod, g_pre_mix, g_post_mix, g_pre_ffn, g_post_ffn,
           w_in, g_q_a, w_q_b, g_kv_a, w_kv_b, g_sgu, w_spatial, b_spatial, g_out_a, g_out_b, w_out,
           w_gate, w_up, w_down):
    layers = [
        _layer_params(l, w_in, g_q_a, w_q_b, g_kv_a, w_kv_b, g_sgu, w_spatial, b_spatial, g_out_a, g_out_b, w_out,
                      g_pre_mix, g_post_mix, g_pre_ffn, g_post_ffn, w_gate, w_up, w_down)
        for l in range(DEPTH)
    ]
    nb_p, nb_s = c_prompt.shape[0], c_sample.shape[0]
    rows = -(-(nb_p + nb_s) // 8) * 8
    c_all = jnp.concatenate([c_prompt, c_sample, jnp.zeros((rows - nb_p - nb_s, D_MODEL), F32)], axis=0)
    mod = _modulation(c_all, w_mod, b_mod)
    y_prompt = _trunk(x_prompt, mod[:, :nb_p], layers)
    y_sample = _trunk(x_sample, mod[:, nb_p:nb_p + nb_s], layers)
    return (y_prompt, y_sample)
```

```python
import functools
import math

import jax
import jax.numpy as jnp
from jax import lax
from jax.experimental import pallas as pl
from jax.experimental.pallas import tpu as pltpu

D_MODEL = 1024
DEPTH = 2
N_HEADS = 8
QK_NOPE = 64
QK_ROPE = 32
ROPE_HALF = QK_ROPE // 2
V_DIM = 64
Q_LORA = 384
KV_LORA = 256
D_A = N_HEADS * V_DIM
D_B = 512
G_B = 8
C_B = D_B // G_B
CHUNK = 128
D_FF = 2816
EPS = 1e-6
N_MOD = 6
ROPE_THETA = 10000.0

LANES = 128
SLAB = LANES
V_SLAB = 80
HEADS_PER_STEP = 4
Q_TILE = 1024
KV_CHUNK = 2048
INIT_KEYS = 128
STAB_ROW = QK_NOPE + QK_ROPE
STAB_ROWS = 16
GROWTH_LIMIT = 64.0
IN_COLS_PADDED = Q_LORA + KV_LORA + 2 * D_B + SLAB
SUBTILE_ROWS = 256
MXU_DEPTH = 256
FF_PIECES = (0, 6 * MXU_DEPTH, D_FF)
VMEM_LIMIT = 56 * 1024 * 1024

BF16 = jnp.bfloat16
F32 = jnp.float32
NEG_BIG = -1e30
Q_SCALE = math.log2(math.e) / math.sqrt(QK_NOPE + QK_ROPE)


def _token_tile(seq, tile=512):
    return min(seq, tile)


def _rms(x, g):
    r = lax.rsqrt(jnp.mean(x * x, axis=-1, keepdims=True) + EPS)
    return x * r * g


def _dot(a, b):
    return jnp.dot(a, b, preferred_element_type=F32)


def _dot_nt(a, b):
    return lax.dot_general(a, b, (((1,), (1,)), ((), ())), preferred_element_type=F32)


def _mod_kernel(c_ref, w_ref, b_ref, o_ref):
    cs = jax.nn.silu(c_ref[...])
    o_ref[0] = _dot(cs.astype(BF16), w_ref[0].astype(BF16)) + b_ref[0]


def _modulation(c_all, w_mod, b_mod):
    rows = c_all.shape[0]
    n = w_mod.shape[-1]
    tn = 1536
    return pl.pallas_call(
        _mod_kernel,
        out_shape=jax.ShapeDtypeStruct((DEPTH, rows, n), F32),
        grid=(DEPTH, n // tn),
        in_specs=[
            pl.BlockSpec((rows, D_MODEL), lambda l, j: (0, 0)),
            pl.BlockSpec((1, D_MODEL, tn), lambda l, j: (l, 0, j)),
            pl.BlockSpec((1, 1, tn), lambda l, j: (l, 0, j)),
        ],
        out_specs=pl.BlockSpec((1, rows, tn), lambda l, j: (l, 0, j)),
        compiler_params=pltpu.CompilerParams(
            dimension_semantics=("arbitrary", "arbitrary"), vmem_limit_bytes=VMEM_LIMIT),
        name="adaln_modulation",
    )(c_all, w_mod, b_mod.reshape(DEPTH, 1, n))


def _rope_slab(x, c, s_lo, s_hi):
    return x * c + pltpu.roll(x, SLAB - ROPE_HALF, 1) * s_lo + pltpu.roll(x, ROPE_HALF, 1) * s_hi


def _premix_kernel(x_ref, mod_ref, gpre_ref, win_ref, gq_ref, wqt_ref, gkv_ref, wk_ref, wvt_ref, gsgu_ref, ws_ref,
                   bs_ref, goutb_ref, ropek_ref, ropeq_ref, qt_out, k_out, vt_out, sg_out, mixed_sc):
    shift = mod_ref[0, :, 0:D_MODEL]
    scale = mod_ref[0, :, D_MODEL:2 * D_MODEL]
    t = x_ref.shape[1]
    n_sub = t // SUBTILE_ROWS if t % SUBTILE_ROWS == 0 else 1
    blocks = [slice(i * t // n_sub, (i + 1) * t // n_sub) for i in range(n_sub)]
    o_kv, o_u, o_v, o_kr = Q_LORA, Q_LORA + KV_LORA, Q_LORA + KV_LORA + D_B, Q_LORA + KV_LORA + 2 * D_B

    z = []
    for r in blocks:
        h = _rms(x_ref[0, r], gpre_ref[...]) * (1.0 + scale) + shift
        z.append(_dot(h.astype(BF16), win_ref[...]))

    lo0, hi0, end = QK_NOPE, QK_NOPE + ROPE_HALF, QK_NOPE + QK_ROPE
    qt = [_dot_nt(wqt_ref[...], _rms(zi[:, 0:Q_LORA], gq_ref[...]).astype(BF16)) for zi in z]
    kvn = [_rms(zi[:, o_kv:o_u], gkv_ref[...]).astype(BF16) for zi in z]
    kk = [_dot(v, wk_ref[...]) for v in kvn]
    vt = [_dot_nt(wvt_ref[...], v) for v in kvn]
    ones_lane = (lax.broadcasted_iota(jnp.int32, (1, SLAB), 1) == STAB_ROW).astype(F32)
    ones_row = (lax.broadcasted_iota(jnp.int32, (V_SLAB, 1), 0) == V_DIM).astype(F32)
    for r, zi, qti, kki, vti in zip(blocks, z, qt, kk, vt):
        cos_t, sin_t = ropeq_ref[0, :, r], ropeq_ref[1, :, r]
        for hd in range(N_HEADS):
            base = hd * SLAB
            x_lo = qti[base + lo0:base + hi0]
            x_hi = qti[base + hi0:base + end]
            slab = jnp.concatenate([
                qti[base:base + lo0] * Q_SCALE,
                (x_lo * cos_t - x_hi * sin_t) * Q_SCALE,
                (x_lo * sin_t + x_hi * cos_t) * Q_SCALE,
                qti[base + end:base + SLAB],
            ], axis=0)
            qt_out[0, base:base + SLAB, r] = slab.astype(BF16)
        kr = _rope_slab(zi[:, o_kr:o_kr + SLAB], ropek_ref[0, r], ropek_ref[1, r], ropek_ref[2, r]) + ones_lane
        for hd in range(N_HEADS):
            sl = slice(hd * SLAB, (hd + 1) * SLAB)
            k_out[0, r, sl] = (kki[:, sl] + kr).astype(BF16)
        for hd in range(N_HEADS):
            base = hd * V_SLAB
            vt_out[0, base:base + V_SLAB, r] = (vti[base:base + V_SLAB] + ones_row).astype(BF16)

    low = lax.broadcasted_iota(jnp.int32, (CHUNK, SLAB), 1) < C_B
    for r, zi in zip(blocks, z):
        gu = jax.nn.gelu(zi[:, o_u:o_v])
        vn = _rms(jax.nn.gelu(zi[:, o_v:o_kr]), gsgu_ref[...])
        for c in range((r.stop - r.start) // CHUNK):
            rows = slice(c * CHUNK, (c + 1) * CHUNK)
            out_rows = slice(r.start + c * CHUNK, r.start + (c + 1) * CHUNK)
            for j in range(D_B // SLAB):
                cols = slice(j * SLAB, (j + 1) * SLAB)
                slab = vn[rows, cols]
                rhs = jnp.concatenate([jnp.where(low, slab, 0.0), jnp.where(low, 0.0, slab)], axis=0).astype(BF16)
                mixed_sc[out_rows, cols] = _dot(ws_ref[j], rhs) + bs_ref[:, cols]
        sgu = gu * mixed_sc[r]
        sg_out[0, r] = _rms(sgu, goutb_ref[...]).astype(BF16)


def _layer_spec(shape, l, **kw):
    return pl.BlockSpec((None,) + tuple(shape), lambda bi, i: (l,) + (0,) * len(shape), **kw)


def _mod_spec(l, row0):
    return pl.BlockSpec((None, 1, 1, N_MOD * D_MODEL), lambda bi, i: (l, row0 + bi, 0, 0))


def _premix(x, mod, p, l, row0, rope_k, rope_q):
    b, s, _ = x.shape
    t = _token_tile(s, 1024)
    tok = lambda bi, i: (bi, i, 0)
    tok_t = lambda bi, i: (bi, 0, i)
    n_slab = N_HEADS * SLAB
    layer = functools.partial(_layer_spec, l=l)
    return pl.pallas_call(
        _premix_kernel,
        out_shape=(
            jax.ShapeDtypeStruct((b, n_slab, s), BF16),
            jax.ShapeDtypeStruct((b, s, n_slab), BF16),
            jax.ShapeDtypeStruct((b, N_HEADS * V_SLAB, s), BF16),
            jax.ShapeDtypeStruct((b, s, D_B), BF16),
        ),
        grid=(b, s // t),
        in_specs=[
            pl.BlockSpec((1, t, D_MODEL), tok),
            _mod_spec(l, row0),
            layer((1, D_MODEL)),
            layer((D_MODEL, IN_COLS_PADDED)),
            layer((1, Q_LORA)),
            layer((n_slab, Q_LORA)),
            layer((1, KV_LORA)),
            layer((KV_LORA, n_slab)),
            layer((N_HEADS * V_SLAB, KV_LORA)),
            layer((1, D_B)),
            layer((D_B // SLAB, CHUNK, 2 * CHUNK)),
            layer((CHUNK, D_B)),
            layer((1, D_B)),
            pl.BlockSpec((3, t, SLAB), lambda bi, i: (0, i, 0)),
            pl.BlockSpec((2, ROPE_HALF, t), lambda bi, i: (0, 0, i)),
        ],
        out_specs=(
            pl.BlockSpec((1, n_slab, t), tok_t),
            pl.BlockSpec((1, t, n_slab), tok),
            pl.BlockSpec((1, N_HEADS * V_SLAB, t), tok_t),
            pl.BlockSpec((1, t, D_B), tok),
        ),
        scratch_shapes=[pltpu.VMEM((t, D_B), F32)],
        compiler_params=pltpu.CompilerParams(
            dimension_semantics=("arbitrary", "arbitrary"), vmem_limit_bytes=VMEM_LIMIT),
        name="premix",
    )(x, mod, p["g_pre_mix"], p["w_in"], p["g_q_a"], p["w_qt"], p["g_kv_a"], p["w_k"], p["w_vt"], p["g_sgu"],
      p["w_spatial"], p["b_spatial"], p["g_out_b"], rope_k, rope_q)


def _round_up_bf16(x):
    return (x + jnp.abs(x) * (1.0 / 128.0)).astype(BF16).astype(F32)


def _attn_kernel(qt_ref, k_ref, vt_ref, o_ref, qs_sc, m_sc, acc_sc, *, tk):
    n_chunks = k_ref.shape[1] // tk
    tq = qs_sc.shape[-1]
    heads = range(HEADS_PER_STEP)
    pad_rows = jnp.zeros((STAB_ROWS - 1, tq), BF16)

    def set_stabiliser(hd, m):
        qs_sc[hd, STAB_ROW:STAB_ROW + STAB_ROWS, :] = jnp.concatenate([(-m).astype(BF16), pad_rows], axis=0)

    def chunk(c, src, dst):
        start = pl.multiple_of(c * tk, tk)
        worst = None
        for hd in heads:
            sl = slice(hd * SLAB, (hd + 1) * SLAB)
            m_old = m_sc[src, hd]
            st = _dot(k_ref[0, pl.ds(start, tk), sl], qs_sc[hd])
            mx = jnp.max(st, axis=0, keepdims=True)
            vt = vt_ref[0, hd * V_SLAB:(hd + 1) * V_SLAB, pl.ds(start, tk)]
            pv = _dot(vt, jnp.exp2(st).astype(BF16))
            m_new = jnp.where(mx > 0.0, _round_up_bf16(m_old + mx), m_old)
            acc_sc[dst, hd] = (acc_sc[src, hd] + pv) * jnp.exp2(m_old - m_new)
            m_sc[dst, hd] = m_new
            top = jnp.max(mx)
            worst = top if worst is None else jnp.maximum(worst, top)
        return worst

    def step(c, src, dst):
        worst = chunk(c, src, dst)

        @pl.when(worst > GROWTH_LIMIT)
        def _():
            for hd in heads:
                m_new = m_sc[dst, hd]
                acc_sc[src, hd] = acc_sc[src, hd] * jnp.exp2(m_sc[src, hd] - m_new)
                m_sc[src, hd] = m_new
                set_stabiliser(hd, m_new)
            chunk(c, src, dst)

        for hd in heads:
            set_stabiliser(hd, m_sc[dst, hd])

    for hd in heads:
        sl = slice(hd * SLAB, (hd + 1) * SLAB)
        qs_sc[hd] = qt_ref[0, sl, :]
        m0 = _round_up_bf16(jnp.max(_dot(k_ref[0, 0:INIT_KEYS, sl], qs_sc[hd]), axis=0, keepdims=True))
        m_sc[0, hd] = m0
        acc_sc[0, hd] = jnp.zeros(acc_sc.shape[2:], F32)
        set_stabiliser(hd, m0)

    def body(c, carry):
        for par in range(2):
            @pl.when(c % 2 == par)
            def _():
                step(c, par, 1 - par)
        return carry

    lax.fori_loop(0, n_chunks, body, 0)
    last = n_chunks % 2
    outs = []
    for hd in heads:
        acc = acc_sc[last, hd]
        outs.append(acc[0:V_DIM] / acc[V_DIM:V_DIM + 1])
    o_ref[0] = jnp.concatenate(outs, axis=0).T.astype(o_ref.dtype)


def _attention(qt, k, vt):
    b, s, _ = k.shape
    tq = min(s, Q_TILE)
    tk = min(s, KV_CHUNK)
    assert s % tq == 0 and s % tk == 0 and tk % LANES == 0 and s >= INIT_KEYS
    w = HEADS_PER_STEP * SLAB
    return pl.pallas_call(
        functools.partial(_attn_kernel, tk=tk),
        out_shape=jax.ShapeDtypeStruct((b, s, D_A), BF16),
        grid=(b, N_HEADS // HEADS_PER_STEP, s // tq),
        in_specs=[
            pl.BlockSpec((1, w, tq), lambda bi, hp, i: (bi, hp, i)),
            pl.BlockSpec((1, s, w), lambda bi, hp, i: (bi, 0, hp)),
            pl.BlockSpec((1, HEADS_PER_STEP * V_SLAB, s), lambda bi, hp, i: (bi, hp, 0)),
        ],
        out_specs=pl.BlockSpec((1, tq, HEADS_PER_STEP * V_DIM), lambda bi, hp, i: (bi, i, hp)),
        scratch_shapes=[
            pltpu.VMEM((HEADS_PER_STEP, SLAB, tq), BF16),
            pltpu.VMEM((2, HEADS_PER_STEP, 1, tq), F32),
            pltpu.VMEM((2, HEADS_PER_STEP, V_SLAB, tq), F32),
        ],
        compiler_params=pltpu.CompilerParams(
            dimension_semantics=("arbitrary", "arbitrary", "arbitrary"), vmem_limit_bytes=VMEM_LIMIT,
        ),
        name="mla_attention",
    )(qt, k, vt)


def _postmix_kernel(x_ref, mod_ref, attn_ref, sg_ref, gouta_ref, wout_ref, gpost_ref, gpre_ref, wgate_ref,
                    wup_ref, wdown_ref, gpostf_ref, o_ref):
    gate1 = mod_ref[0, :, 2 * D_MODEL:3 * D_MODEL]
    shift2 = mod_ref[0, :, 3 * D_MODEL:4 * D_MODEL]
    scale2 = mod_ref[0, :, 4 * D_MODEL:5 * D_MODEL]
    gate2 = mod_ref[0, :, 5 * D_MODEL:6 * D_MODEL]
    t = x_ref.shape[1]
    n_sub = t // SUBTILE_ROWS if t % SUBTILE_ROWS == 0 else 1
    blocks = [slice(i * t // n_sub, (i + 1) * t // n_sub) for i in range(n_sub)]
    merged = [jnp.concatenate([_rms(attn_ref[0, r].astype(F32), gouta_ref[...]).astype(BF16), sg_ref[0, r]], axis=1)
              for r in blocks]
    m = [_dot(v, wout_ref[...]) for v in merged]
    x1 = [x_ref[0, r] + gate1 * _rms(v, gpost_ref[...]) for r, v in zip(blocks, m)]
    h = [(_rms(v, gpre_ref[...]) * (1.0 + scale2) + shift2).astype(BF16) for v in x1]
    f = [None] * n_sub
    for lo, hi in zip(FF_PIECES[:-1], FF_PIECES[1:]):
        cols = slice(lo, hi)
        for i in range(n_sub):
            act = (jax.nn.silu(_dot(h[i], wgate_ref[:, cols])) * _dot(h[i], wup_ref[:, cols])).astype(BF16)
            part = _dot(act, wdown_ref[cols, :])
            f[i] = part if f[i] is None else f[i] + part
    for r, v1, vf in zip(blocks, x1, f):
        o_ref[0, r] = v1 + gate2 * _rms(vf, gpostf_ref[...])


def _postmix(x, mod, attn, sg, p, l, row0):
    b, s, _ = x.shape
    t = _token_tile(s, 1024)
    tok = lambda bi, i: (bi, i, 0)
    layer = functools.partial(_layer_spec, l=l)
    resident = functools.partial(_layer_spec, l=l, pipeline_mode=pl.Buffered(1))
    return pl.pallas_call(
        _postmix_kernel,
        out_shape=jax.ShapeDtypeStruct(x.shape, F32),
        grid=(b, s // t),
        in_specs=[
            pl.BlockSpec((1, t, D_MODEL), tok),
            _mod_spec(l, row0),
            pl.BlockSpec((1, t, D_A), tok),
            pl.BlockSpec((1, t, D_B), tok),
            layer((1, D_A)),
            resident((D_A + D_B, D_MODEL)),
            layer((1, D_MODEL)),
            layer((1, D_MODEL)),
            resident((D_MODEL, D_FF)),
            resident((D_MODEL, D_FF)),
            resident((D_FF, D_MODEL)),
            layer((1, D_MODEL)),
        ],
        out_specs=pl.BlockSpec((1, t, D_MODEL), tok),
        compiler_params=pltpu.CompilerParams(
            dimension_semantics=("arbitrary", "arbitrary"), vmem_limit_bytes=VMEM_LIMIT),
        name="postmix",
    )(x, mod, attn, sg, p["g_out_a"], p["w_out"], p["g_post_mix"], p["g_pre_ffn"], p["w_gate"], p["w_up"],
      p["w_down"], p["g_post_ffn"])


def _pad_heads(w, width, slab=SLAB):
    lead = w.shape[:-1]
    w = w.reshape(lead + (N_HEADS, width))
    return jnp.pad(w, ((0, 0),) * (len(lead) + 1) + ((0, slab - width),)).reshape(lead + (N_HEADS * slab,))


def _prep_params(w_in, g_q_a, w_q_b, g_kv_a, w_kv_b, g_sgu, w_spatial, b_spatial, g_out_a, g_out_b, w_out,
                 g_pre_mix, g_post_mix, g_pre_ffn, g_post_ffn, w_gate, w_up, w_down):
    o_kr = Q_LORA + KV_LORA
    o_u = o_kr + QK_ROPE
    kr_slab = jnp.pad(w_in[:, :, o_kr:o_u], ((0, 0), (0, 0), (QK_NOPE, SLAB - QK_NOPE - QK_ROPE)))
    w_in_p = jnp.concatenate([w_in[:, :, :o_kr], w_in[:, :, o_u:], kr_slab], axis=2)
    kvw = w_kv_b.reshape(DEPTH, KV_LORA, N_HEADS, QK_NOPE + V_DIM)
    w_k = _pad_heads(kvw[..., :QK_NOPE].reshape(DEPTH, KV_LORA, N_HEADS * QK_NOPE), QK_NOPE)
    w_v = _pad_heads(kvw[..., QK_NOPE:].reshape(DEPTH, KV_LORA, N_HEADS * V_DIM), V_DIM, V_SLAB)
    ws_pairs = jnp.concatenate([w_spatial[:, 0::2], w_spatial[:, 1::2]], axis=3)
    bias = jnp.repeat(jnp.swapaxes(b_spatial, 1, 2), C_B, axis=2)
    row = lambda g: g.reshape(DEPTH, 1, -1)
    t_bf16 = lambda w: jnp.swapaxes(w, 1, 2).astype(BF16)
    return dict(
        g_pre_mix=row(g_pre_mix), w_in=w_in_p.astype(BF16), g_q_a=row(g_q_a),
        w_qt=t_bf16(_pad_heads(w_q_b, QK_NOPE + QK_ROPE)), g_kv_a=row(g_kv_a),
        w_k=w_k.astype(BF16), w_vt=t_bf16(w_v), g_sgu=row(g_sgu),
        w_spatial=ws_pairs.astype(BF16), b_spatial=bias, g_out_b=row(g_out_b), g_out_a=row(g_out_a),
        w_out=w_out.astype(BF16), g_post_mix=row(g_post_mix), g_pre_ffn=row(g_pre_ffn),
        w_gate=w_gate.astype(BF16), w_up=w_up.astype(BF16), w_down=w_down.astype(BF16),
        g_post_ffn=row(g_post_ffn))


def _rope_tables(seq):
    pos = jnp.arange(seq, dtype=F32)
    inv_freq = ROPE_THETA ** (-jnp.arange(0, QK_ROPE, 2, dtype=F32) / QK_ROPE)
    ang = pos[:, None] * inv_freq[None, :]
    cos, sin = jnp.cos(ang), jnp.sin(ang)
    zeros = lambda n: jnp.zeros((seq, n), F32)
    tail = SLAB - QK_NOPE - QK_ROPE
    c = jnp.concatenate([jnp.ones((seq, QK_NOPE), F32), cos, cos, zeros(tail)], axis=1)
    s_lo = jnp.concatenate([zeros(QK_NOPE), -sin, zeros(ROPE_HALF + tail)], axis=1)
    s_hi = jnp.concatenate([zeros(QK_NOPE + ROPE_HALF), sin, zeros(tail)], axis=1)
    return jnp.stack([c, s_lo, s_hi]), jnp.stack([cos.T, sin.T])


def _trunk(x, mod, row0, params, rope_k, rope_q):
    for l in range(DEPTH):
        qt, k, vt, sg = _premix(x, mod, params, l, row0, rope_k, rope_q)
        attn = _attention(qt, k, vt)
        x = _postmix(x, mod, attn, sg, params, l, row0)
    return x


def kernel(x_prompt, x_sample, c_prompt, c_sample, w_mod, b_mod, g_pre_mix, g_post_mix, g_pre_ffn, g_post_ffn,
           w_in, g_q_a, w_q_b, g_kv_a, w_kv_b, g_sgu, w_spatial, b_spatial, g_out_a, g_out_b, w_out,
           w_gate, w_up, w_down):
    params = _prep_params(w_in, g_q_a, w_q_b, g_kv_a, w_kv_b, g_sgu, w_spatial, b_spatial, g_out_a, g_out_b, w_out,
                          g_pre_mix, g_post_mix, g_pre_ffn, g_post_ffn, w_gate, w_up, w_down)
    nb_p, nb_s = c_prompt.shape[0], c_sample.shape[0]
    rows = -(-(nb_p + nb_s) // 8) * 8
    c_all = jnp.concatenate([c_prompt, c_sample, jnp.zeros((rows - nb_p - nb_s, D_MODEL), F32)], axis=0)
    mod = _modulation(c_all, w_mod, b_mod).reshape(DEPTH, rows, 1, N_MOD * D_MODEL)
    rope_k, rope_q = _rope_tables(max(x_prompt.shape[1], x_sample.shape[1]))
    y_prompt = _trunk(x_prompt, mod, 0, params, rope_k, rope_q)
    y_sample = _trunk(x_sample, mod, nb_p, params, rope_k, rope_q)
    return (y_prompt, y_sample)
```

```python
import functools
import math

import jax
import jax.numpy as jnp
from jax import lax
from jax.experimental import pallas as pl
from jax.experimental.pallas import tpu as pltpu

D_MODEL = 1024
DEPTH = 2
N_HEADS = 8
QK_NOPE = 64
QK_ROPE = 32
ROPE_HALF = QK_ROPE // 2
V_DIM = 64
Q_LORA = 384
KV_LORA = 256
D_A = N_HEADS * V_DIM
D_B = 512
G_B = 8
C_B = D_B // G_B
CHUNK = 128
D_FF = 2816
EPS = 1e-6
N_MOD = 6
ROPE_THETA = 10000.0

LANES = 128
SLAB = LANES
V_SLAB = 80
HEADS_PER_STEP = 4
Q_TILE = 1024
KV_CHUNK = 2048
STAB_ROW = QK_NOPE + QK_ROPE
STAB_ROWS = 16
GROWTH_LIMIT = 64.0
IN_COLS_PADDED = Q_LORA + KV_LORA + 2 * D_B + SLAB
SUBTILE_ROWS = 256
MXU_DEPTH = 256
FF_PIECES = (0, 6 * MXU_DEPTH, D_FF)
VMEM_LIMIT = 56 * 1024 * 1024

BF16 = jnp.bfloat16
F32 = jnp.float32
NEG_BIG = -1e30
Q_SCALE = math.log2(math.e) / math.sqrt(QK_NOPE + QK_ROPE)


def _token_tile(seq, tile=512):
    return min(seq, tile)


def _rms(x, g):
    r = lax.rsqrt(jnp.mean(x * x, axis=-1, keepdims=True) + EPS)
    return x * r * g


def _dot(a, b):
    return jnp.dot(a, b, preferred_element_type=F32)


def _dot_nt(a, b):
    return lax.dot_general(a, b, (((1,), (1,)), ((), ())), preferred_element_type=F32)


def _mod_kernel(c_ref, w_ref, b_ref, o_ref):
    cs = jax.nn.silu(c_ref[...])
    o_ref[0] = _dot(cs.astype(BF16), w_ref[0].astype(BF16)) + b_ref[0]


def _modulation(c_all, w_mod, b_mod):
    rows = c_all.shape[0]
    n = w_mod.shape[-1]
    tn = 1536
    return pl.pallas_call(
        _mod_kernel,
        out_shape=jax.ShapeDtypeStruct((DEPTH, rows, n), F32),
        grid=(DEPTH, n // tn),
        in_specs=[
            pl.BlockSpec((rows, D_MODEL), lambda l, j: (0, 0)),
            pl.BlockSpec((1, D_MODEL, tn), lambda l, j: (l, 0, j)),
            pl.BlockSpec((1, 1, tn), lambda l, j: (l, 0, j)),
        ],
        out_specs=pl.BlockSpec((1, rows, tn), lambda l, j: (l, 0, j)),
        compiler_params=pltpu.CompilerParams(
            dimension_semantics=("arbitrary", "arbitrary"), vmem_limit_bytes=VMEM_LIMIT),
        name="adaln_modulation",
    )(c_all, w_mod, b_mod.reshape(DEPTH, 1, n))


def _rope_slab(x, c, s_lo, s_hi):
    return x * c + pltpu.roll(x, SLAB - ROPE_HALF, 1) * s_lo + pltpu.roll(x, ROPE_HALF, 1) * s_hi


def _premix_kernel(x_ref, mod_ref, gpre_ref, win_ref, gq_ref, wqt_ref, gkv_ref, wk_ref, wvt_ref, gsgu_ref, ws_ref,
                   bs_ref, goutb_ref, ropek_ref, ropeq_ref, qt_out, k_out, vt_out, sg_out, mixed_sc):
    shift = mod_ref[0, :, 0:D_MODEL]
    scale = mod_ref[0, :, D_MODEL:2 * D_MODEL]
    t = x_ref.shape[1]
    n_sub = t // SUBTILE_ROWS if t % SUBTILE_ROWS == 0 else 1
    blocks = [slice(i * t // n_sub, (i + 1) * t // n_sub) for i in range(n_sub)]
    o_kv, o_u, o_v, o_kr = Q_LORA, Q_LORA + KV_LORA, Q_LORA + KV_LORA + D_B, Q_LORA + KV_LORA + 2 * D_B

    z = []
    for r in blocks:
        h = _rms(x_ref[0, r], gpre_ref[...]) * (1.0 + scale) + shift
        z.append(_dot(h.astype(BF16), win_ref[...]))

    lo0, hi0, end = QK_NOPE, QK_NOPE + ROPE_HALF, QK_NOPE + QK_ROPE
    qt = [_dot_nt(wqt_ref[...], _rms(zi[:, 0:Q_LORA], gq_ref[...]).astype(BF16)) for zi in z]
    kvn = [_rms(zi[:, o_kv:o_u], gkv_ref[...]).astype(BF16) for zi in z]
    kk = [_dot(v, wk_ref[...]) for v in kvn]
    vt = [_dot_nt(wvt_ref[...], v) for v in kvn]
    ones_lane = (lax.broadcasted_iota(jnp.int32, (1, SLAB), 1) == STAB_ROW).astype(F32)
    ones_row = (lax.broadcasted_iota(jnp.int32, (V_SLAB, 1), 0) == V_DIM).astype(F32)
    for r, zi, qti, kki, vti in zip(blocks, z, qt, kk, vt):
        cos_t, sin_t = ropeq_ref[0, :, r], ropeq_ref[1, :, r]
        for hd in range(N_HEADS):
            base = hd * SLAB
            x_lo = qti[base + lo0:base + hi0]
            x_hi = qti[base + hi0:base + end]
            slab = jnp.concatenate([
                qti[base:base + lo0] * Q_SCALE,
                (x_lo * cos_t - x_hi * sin_t) * Q_SCALE,
                (x_lo * sin_t + x_hi * cos_t) * Q_SCALE,
                qti[base + end:base + SLAB],
            ], axis=0)
            qt_out[0, base:base + SLAB, r] = slab.astype(BF16)
        kr = _rope_slab(zi[:, o_kr:o_kr + SLAB], ropek_ref[0, r], ropek_ref[1, r], ropek_ref[2, r]) + ones_lane
        for hd in range(N_HEADS):
            sl = slice(hd * SLAB, (hd + 1) * SLAB)
            k_out[0, r, sl] = (kki[:, sl] + kr).astype(BF16)
        for hd in range(N_HEADS):
            base = hd * V_SLAB
            vt_out[0, base:base + V_SLAB, r] = (vti[base:base + V_SLAB] + ones_row).astype(BF16)

    low = lax.broadcasted_iota(jnp.int32, (CHUNK, SLAB), 1) < C_B
    for r, zi in zip(blocks, z):
        gu = jax.nn.gelu(zi[:, o_u:o_v])
        vn = _rms(jax.nn.gelu(zi[:, o_v:o_kr]), gsgu_ref[...])
        n_chunks = (r.stop - r.start) // CHUNK
        pair = 2 if n_chunks % 2 == 0 else 1
        for c in range(0, n_chunks, pair):
            for j in range(D_B // SLAB):
                cols = slice(j * SLAB, (j + 1) * SLAB)
                rhs = []
                for cc in range(c, c + pair):
                    slab = vn[cc * CHUNK:(cc + 1) * CHUNK, cols]
                    rhs.append(jnp.concatenate([jnp.where(low, slab, 0.0), jnp.where(low, 0.0, slab)], axis=0))
                mixed = _dot(ws_ref[j], jnp.concatenate(rhs, axis=1).astype(BF16))
                for k, cc in enumerate(range(c, c + pair)):
                    out_rows = slice(r.start + cc * CHUNK, r.start + (cc + 1) * CHUNK)
                    mixed_sc[out_rows, cols] = mixed[:, k * SLAB:(k + 1) * SLAB] + bs_ref[:, cols]
        sgu = gu * mixed_sc[r]
        sg_out[0, r] = _rms(sgu, goutb_ref[...]).astype(BF16)


def _layer_spec(shape, l, **kw):
    return pl.BlockSpec((None,) + tuple(shape), lambda bi, i: (l,) + (0,) * len(shape), **kw)


def _mod_spec(l, row0):
    return pl.BlockSpec((None, 1, 1, N_MOD * D_MODEL), lambda bi, i: (l, row0 + bi, 0, 0))


def _premix(x, mod, p, l, row0, rope_k, rope_q):
    b, s, _ = x.shape
    t = _token_tile(s, 1024)
    tok = lambda bi, i: (bi, i, 0)
    tok_t = lambda bi, i: (bi, 0, i)
    n_slab = N_HEADS * SLAB
    layer = functools.partial(_layer_spec, l=l)
    return pl.pallas_call(
        _premix_kernel,
        out_shape=(
            jax.ShapeDtypeStruct((b, n_slab, s), BF16),
            jax.ShapeDtypeStruct((b, s, n_slab), BF16),
            jax.ShapeDtypeStruct((b, N_HEADS * V_SLAB, s), BF16),
            jax.ShapeDtypeStruct((b, s, D_B), BF16),
        ),
        grid=(b, s // t),
        in_specs=[
            pl.BlockSpec((1, t, D_MODEL), tok),
            _mod_spec(l, row0),
            layer((1, D_MODEL)),
            layer((D_MODEL, IN_COLS_PADDED)),
            layer((1, Q_LORA)),
            layer((n_slab, Q_LORA)),
            layer((1, KV_LORA)),
            layer((KV_LORA, n_slab)),
            layer((N_HEADS * V_SLAB, KV_LORA)),
            layer((1, D_B)),
            layer((D_B // SLAB, CHUNK, 2 * CHUNK)),
            layer((CHUNK, D_B)),
            layer((1, D_B)),
            pl.BlockSpec((3, t, SLAB), lambda bi, i: (0, i, 0)),
            pl.BlockSpec((2, ROPE_HALF, t), lambda bi, i: (0, 0, i)),
        ],
        out_specs=(
            pl.BlockSpec((1, n_slab, t), tok_t),
            pl.BlockSpec((1, t, n_slab), tok),
            pl.BlockSpec((1, N_HEADS * V_SLAB, t), tok_t),
            pl.BlockSpec((1, t, D_B), tok),
        ),
        scratch_shapes=[pltpu.VMEM((t, D_B), F32)],
        compiler_params=pltpu.CompilerParams(
            dimension_semantics=("arbitrary", "arbitrary"), vmem_limit_bytes=VMEM_LIMIT),
        name="premix",
    )(x, mod, p["g_pre_mix"], p["w_in"], p["g_q_a"], p["w_qt"], p["g_kv_a"], p["w_k"], p["w_vt"], p["g_sgu"],
      p["w_spatial"], p["b_spatial"], p["g_out_b"], rope_k, rope_q)


def _round_up_bf16(x):
    return (x + jnp.abs(x) * (1.0 / 128.0)).astype(BF16).astype(F32)


def _attn_kernel(qt_ref, k_ref, vt_ref, o_ref, qs_sc, m_sc, acc_sc, p_sc, *, tk):
    n_chunks = k_ref.shape[1] // tk
    tq = qs_sc.shape[-1]
    heads = range(HEADS_PER_STEP)
    pad_rows = jnp.zeros((STAB_ROWS - 1, tq), BF16)

    def set_stabiliser(hd, m):
        qs_sc[hd, STAB_ROW:STAB_ROW + STAB_ROWS, :] = jnp.concatenate([(-m).astype(BF16), pad_rows], axis=0)

    def chunk(c, src, dst):
        start = pl.multiple_of(c * tk, tk)
        floor = jnp.where(c == 0, -GROWTH_LIMIT, NEG_BIG)
        hi = lo = None
        for hd in heads:
            sl = slice(hd * SLAB, (hd + 1) * SLAB)
            m_old = m_sc[src, hd]
            st = _dot(k_ref[0, pl.ds(start, tk), sl], qs_sc[hd])
            mx = jnp.max(st, axis=0, keepdims=True)
            vt = vt_ref[0, hd * V_SLAB:(hd + 1) * V_SLAB, pl.ds(start, tk)]
            p_sc[hd % 2] = jnp.exp2(st).astype(BF16)
            pv = _dot(vt, p_sc[hd % 2])
            m_new = jnp.where((mx > 0.0) | (mx < floor), _round_up_bf16(m_old + mx), m_old)
            acc_sc[dst, hd] = (acc_sc[src, hd] + pv) * jnp.exp2(jnp.minimum(m_old - m_new, 0.0))
            m_sc[dst, hd] = m_new
            top, bot = jnp.max(mx), jnp.min(mx)
            hi = top if hi is None else jnp.maximum(hi, top)
            lo = bot if lo is None else jnp.minimum(lo, bot)
        return hi, lo, floor

    def step(c, src, dst):
        hi, lo, floor = chunk(c, src, dst)

        @pl.when((hi > GROWTH_LIMIT) | (lo < floor))
        def _():
            for hd in heads:
                m_new = m_sc[dst, hd]
                acc_sc[src, hd] = acc_sc[src, hd] * jnp.exp2(jnp.minimum(m_sc[src, hd] - m_new, 0.0))
                m_sc[src, hd] = m_new
                set_stabiliser(hd, m_new)
            chunk(c, src, dst)

        for hd in heads:
            set_stabiliser(hd, m_sc[dst, hd])

    for hd in heads:
        qs_sc[hd] = qt_ref[0, hd * SLAB:(hd + 1) * SLAB, :]
        m_sc[0, hd] = jnp.zeros(m_sc.shape[2:], F32)
        acc_sc[0, hd] = jnp.zeros(acc_sc.shape[2:], F32)

    def body(c, carry):
        for par in range(2):
            @pl.when(c % 2 == par)
            def _():
                step(c, par, 1 - par)
        return carry

    lax.fori_loop(0, n_chunks, body, 0)
    last = n_chunks % 2
    outs = []
    for hd in heads:
        acc = acc_sc[last, hd]
        outs.append(acc[0:V_DIM] / acc[V_DIM:V_DIM + 1])
    o_ref[0] = jnp.concatenate(outs, axis=0).T.astype(o_ref.dtype)


def _attention(qt, k, vt):
    b, s, _ = k.shape
    tq = min(s, Q_TILE)
    tk = min(s, KV_CHUNK)
    assert s % tq == 0 and s % tk == 0 and tk % LANES == 0
    w = HEADS_PER_STEP * SLAB
    return pl.pallas_call(
        functools.partial(_attn_kernel, tk=tk),
        out_shape=jax.ShapeDtypeStruct((b, s, D_A), BF16),
        grid=(b, N_HEADS // HEADS_PER_STEP, s // tq),
        in_specs=[
            pl.BlockSpec((1, w, tq), lambda bi, hp, i: (bi, hp, i)),
            pl.BlockSpec((1, s, w), lambda bi, hp, i: (bi, 0, hp)),
            pl.BlockSpec((1, HEADS_PER_STEP * V_SLAB, s), lambda bi, hp, i: (bi, hp, 0)),
        ],
        out_specs=pl.BlockSpec((1, tq, HEADS_PER_STEP * V_DIM), lambda bi, hp, i: (bi, i, hp)),
        scratch_shapes=[
            pltpu.VMEM((HEADS_PER_STEP, SLAB, tq), BF16),
            pltpu.VMEM((2, HEADS_PER_STEP, 1, tq), F32),
            pltpu.VMEM((2, HEADS_PER_STEP, V_SLAB, tq), F32),
            pltpu.VMEM((2, tk, tq), BF16),
        ],
        compiler_params=pltpu.CompilerParams(
            dimension_semantics=("arbitrary", "arbitrary", "arbitrary"), vmem_limit_bytes=VMEM_LIMIT,
        ),
        name="mla_attention",
    )(qt, k, vt)


def _postmix_kernel(x_ref, mod_ref, attn_ref, sg_ref, gouta_ref, wout_ref, gpost_ref, gpre_ref, wgate_ref,
                    wup_ref, wdown_ref, gpostf_ref, o_ref):
    gate1 = mod_ref[0, :, 2 * D_MODEL:3 * D_MODEL]
    shift2 = mod_ref[0, :, 3 * D_MODEL:4 * D_MODEL]
    scale2 = mod_ref[0, :, 4 * D_MODEL:5 * D_MODEL]
    gate2 = mod_ref[0, :, 5 * D_MODEL:6 * D_MODEL]
    t = x_ref.shape[1]
    n_sub = t // SUBTILE_ROWS if t % SUBTILE_ROWS == 0 else 1
    blocks = [slice(i * t // n_sub, (i + 1) * t // n_sub) for i in range(n_sub)]
    merged = [jnp.concatenate([_rms(attn_ref[0, r].astype(F32), gouta_ref[...]).astype(BF16), sg_ref[0, r]], axis=1)
              for r in blocks]
    m = [_dot(v, wout_ref[...]) for v in merged]
    x1 = [x_ref[0, r] + gate1 * _rms(v, gpost_ref[...]) for r, v in zip(blocks, m)]
    h = [(_rms(v, gpre_ref[...]) * (1.0 + scale2) + shift2).astype(BF16) for v in x1]
    f = [None] * n_sub
    for lo, hi in zip(FF_PIECES[:-1], FF_PIECES[1:]):
        cols = slice(lo, hi)
        for i in range(n_sub):
            act = (jax.nn.silu(_dot(h[i], wgate_ref[:, cols])) * _dot(h[i], wup_ref[:, cols])).astype(BF16)
            part = _dot(act, wdown_ref[cols, :])
            f[i] = part if f[i] is None else f[i] + part
    for r, v1, vf in zip(blocks, x1, f):
        o_ref[0, r] = v1 + gate2 * _rms(vf, gpostf_ref[...])


def _postmix(x, mod, attn, sg, p, l, row0):
    b, s, _ = x.shape
    t = _token_tile(s, 1024)
    tok = lambda bi, i: (bi, i, 0)
    layer = functools.partial(_layer_spec, l=l)
    resident = functools.partial(_layer_spec, l=l, pipeline_mode=pl.Buffered(1))
    return pl.pallas_call(
        _postmix_kernel,
        out_shape=jax.ShapeDtypeStruct(x.shape, F32),
        grid=(b, s // t),
        in_specs=[
            pl.BlockSpec((1, t, D_MODEL), tok),
            _mod_spec(l, row0),
            pl.BlockSpec((1, t, D_A), tok),
            pl.BlockSpec((1, t, D_B), tok),
            layer((1, D_A)),
            resident((D_A + D_B, D_MODEL)),
            layer((1, D_MODEL)),
            layer((1, D_MODEL)),
            resident((D_MODEL, D_FF)),
            resident((D_MODEL, D_FF)),
            resident((D_FF, D_MODEL)),
            layer((1, D_MODEL)),
        ],
        out_specs=pl.BlockSpec((1, t, D_MODEL), tok),
        compiler_params=pltpu.CompilerParams(
            dimension_semantics=("arbitrary", "arbitrary"), vmem_limit_bytes=VMEM_LIMIT),
        name="postmix",
    )(x, mod, attn, sg, p["g_out_a"], p["w_out"], p["g_post_mix"], p["g_pre_ffn"], p["w_gate"], p["w_up"],
      p["w_down"], p["g_post_ffn"])


def _pad_heads(w, width, slab=SLAB):
    lead = w.shape[:-1]
    w = w.reshape(lead + (N_HEADS, width))
    return jnp.pad(w, ((0, 0),) * (len(lead) + 1) + ((0, slab - width),)).reshape(lead + (N_HEADS * slab,))


def _prep_params(w_in, g_q_a, w_q_b, g_kv_a, w_kv_b, g_sgu, w_spatial, b_spatial, g_out_a, g_out_b, w_out,
                 g_pre_mix, g_post_mix, g_pre_ffn, g_post_ffn, w_gate, w_up, w_down):
    o_kr = Q_LORA + KV_LORA
    o_u = o_kr + QK_ROPE
    kr_slab = jnp.pad(w_in[:, :, o_kr:o_u], ((0, 0), (0, 0), (QK_NOPE, SLAB - QK_NOPE - QK_ROPE)))
    w_in_p = jnp.concatenate([w_in[:, :, :o_kr], w_in[:, :, o_u:], kr_slab], axis=2)
    kvw = w_kv_b.reshape(DEPTH, KV_LORA, N_HEADS, QK_NOPE + V_DIM)
    w_k = _pad_heads(kvw[..., :QK_NOPE].reshape(DEPTH, KV_LORA, N_HEADS * QK_NOPE), QK_NOPE)
    w_v = _pad_heads(kvw[..., QK_NOPE:].reshape(DEPTH, KV_LORA, N_HEADS * V_DIM), V_DIM, V_SLAB)
    ws_pairs = jnp.concatenate([w_spatial[:, 0::2], w_spatial[:, 1::2]], axis=3)
    bias = jnp.repeat(jnp.swapaxes(b_spatial, 1, 2), C_B, axis=2)
    row = lambda g: g.reshape(DEPTH, 1, -1)
    t_bf16 = lambda w: jnp.swapaxes(w, 1, 2).astype(BF16)
    return dict(
        g_pre_mix=row(g_pre_mix), w_in=w_in_p.astype(BF16), g_q_a=row(g_q_a),
        w_qt=t_bf16(_pad_heads(w_q_b, QK_NOPE + QK_ROPE)), g_kv_a=row(g_kv_a),
        w_k=w_k.astype(BF16), w_vt=t_bf16(w_v), g_sgu=row(g_sgu),
        w_spatial=ws_pairs.astype(BF16), b_spatial=bias, g_out_b=row(g_out_b), g_out_a=row(g_out_a),
        w_out=w_out.astype(BF16), g_post_mix=row(g_post_mix), g_pre_ffn=row(g_pre_ffn),
        w_gate=w_gate.astype(BF16), w_up=w_up.astype(BF16), w_down=w_down.astype(BF16),
        g_post_ffn=row(g_post_ffn))


def _rope_tables(seq):
    pos = jnp.arange(seq, dtype=F32)
    inv_freq = ROPE_THETA ** (-jnp.arange(0, QK_ROPE, 2, dtype=F32) / QK_ROPE)
    ang = pos[:, None] * inv_freq[None, :]
    cos, sin = jnp.cos(ang), jnp.sin(ang)
    zeros = lambda n: jnp.zeros((seq, n), F32)
    tail = SLAB - QK_NOPE - QK_ROPE
    c = jnp.concatenate([jnp.ones((seq, QK_NOPE), F32), cos, cos, zeros(tail)], axis=1)
    s_lo = jnp.concatenate([zeros(QK_NOPE), -sin, zeros(ROPE_HALF + tail)], axis=1)
    s_hi = jnp.concatenate([zeros(QK_NOPE + ROPE_HALF), sin, zeros(tail)], axis=1)
    return jnp.stack([c, s_lo, s_hi]), jnp.stack([cos.T, sin.T])


def _trunk(x, mod, row0, params, rope_k, rope_q):
    for l in range(DEPTH):
        qt, k, vt, sg = _premix(x, mod, params, l, row0, rope_k, rope_q)
        attn = _attention(qt, k, vt)
        x = _postmix(x, mod, attn, sg, params, l, row0)
    return x


def kernel(x_prompt, x_sample, c_prompt, c_sample, w_mod, b_mod, g_pre_mix, g_post_mix, g_pre_ffn, g_post_ffn,
           w_in, g_q_a, w_q_b, g_kv_a, w_kv_b, g_sgu, w_spatial, b_spatial, g_out_a, g_out_b, w_out,
           w_gate, w_up, w_down):
    params = _prep_params(w_in, g_q_a, w_q_b, g_kv_a, w_kv_b, g_sgu, w_spatial, b_spatial, g_out_a, g_out_b, w_out,
                          g_pre_mix, g_post_mix, g_pre_ffn, g_post_ffn, w_gate, w_up, w_down)
    nb_p, nb_s = c_prompt.shape[0], c_sample.shape[0]
    rows = -(-(nb_p + nb_s) // 8) * 8
    c_all = jnp.concatenate([c_prompt, c_sample, jnp.zeros((rows - nb_p - nb_s, D_MODEL), F32)], axis=0)
    mod = _modulation(c_all, w_mod, b_mod).reshape(DEPTH, rows, 1, N_MOD * D_MODEL)
    rope_k, rope_q = _rope_tables(max(x_prompt.shape[1], x_sample.shape[1]))
    y_prompt = _trunk(x_prompt, mod, 0, params, rope_k, rope_q)
    y_sample = _trunk(x_sample, mod, nb_p, params, rope_k, rope_q)
    return (y_prompt, y_sample)
```

```python
import functools
import math

import jax
import jax.numpy as jnp
from jax import lax
from jax.experimental import pallas as pl
from jax.experimental.pallas import tpu as pltpu

D_MODEL = 1024
DEPTH = 2
N_HEADS = 8
QK_NOPE = 64
QK_ROPE = 32
ROPE_HALF = QK_ROPE // 2
V_DIM = 64
Q_LORA = 384
KV_LORA = 256
D_A = N_HEADS * V_DIM
D_B = 512
G_B = 8
C_B = D_B // G_B
CHUNK = 128
D_FF = 2816
EPS = 1e-6
N_MOD = 6
ROPE_THETA = 10000.0

LANES = 128
SLAB = LANES
V_SLAB = 80
HEADS_PER_STEP = 4
Q_TILE = 1024
KV_CHUNK = 2048
STAB_ROW = QK_NOPE + QK_ROPE
STAB_ROWS = 16
GROWTH_LIMIT = 64.0
IN_COLS_PADDED = Q_LORA + KV_LORA + 2 * D_B + SLAB
SUBTILE_ROWS = 256
MXU_DEPTH = 256
FF_PIECES = (0, 6 * MXU_DEPTH, D_FF)
VMEM_LIMIT = 56 * 1024 * 1024

BF16 = jnp.bfloat16
F32 = jnp.float32
NEG_BIG = -1e30
Q_SCALE = math.log2(math.e) / math.sqrt(QK_NOPE + QK_ROPE)


def _token_tile(seq, tile=512):
    return min(seq, tile)


def _rms(x, g):
    r = lax.rsqrt(jnp.mean(x * x, axis=-1, keepdims=True) + EPS)
    return x * r * g


def _dot(a, b):
    return jnp.dot(a, b, preferred_element_type=F32)


def _dot_nt(a, b):
    return lax.dot_general(a, b, (((1,), (1,)), ((), ())), preferred_element_type=F32)


def _mod_kernel(c_ref, w_ref, b_ref, o_ref):
    cs = jax.nn.silu(c_ref[...])
    o_ref[0] = _dot(cs.astype(BF16), w_ref[0].astype(BF16)) + b_ref[0]


def _modulation(c_all, w_mod, b_mod):
    rows = c_all.shape[0]
    n = w_mod.shape[-1]
    tn = 1536
    return pl.pallas_call(
        _mod_kernel,
        out_shape=jax.ShapeDtypeStruct((DEPTH, rows, n), F32),
        grid=(DEPTH, n // tn),
        in_specs=[
            pl.BlockSpec((rows, D_MODEL), lambda l, j: (0, 0)),
            pl.BlockSpec((1, D_MODEL, tn), lambda l, j: (l, 0, j)),
            pl.BlockSpec((1, 1, tn), lambda l, j: (l, 0, j)),
        ],
        out_specs=pl.BlockSpec((1, rows, tn), lambda l, j: (l, 0, j)),
        compiler_params=pltpu.CompilerParams(
            dimension_semantics=("arbitrary", "arbitrary"), vmem_limit_bytes=VMEM_LIMIT),
        name="adaln_modulation",
    )(c_all, w_mod, b_mod.reshape(DEPTH, 1, n))


def _rope_slab(x, c, s_lo, s_hi):
    return x * c + pltpu.roll(x, SLAB - ROPE_HALF, 1) * s_lo + pltpu.roll(x, ROPE_HALF, 1) * s_hi


def _premix_kernel(x_ref, mod_ref, gpre_ref, win_ref, gq_ref, wqt_ref, gkv_ref, wk_ref, wvt_ref, gsgu_ref, ws_ref,
                   bs_ref, goutb_ref, ropek_ref, ropeq_ref, qt_out, k_out, vt_out, sg_out, mixed_sc):
    shift = mod_ref[0, :, 0:D_MODEL]
    scale = mod_ref[0, :, D_MODEL:2 * D_MODEL]
    t = x_ref.shape[1]
    n_sub = t // SUBTILE_ROWS if t % SUBTILE_ROWS == 0 else 1
    blocks = [slice(i * t // n_sub, (i + 1) * t // n_sub) for i in range(n_sub)]
    o_kv, o_u, o_v, o_kr = Q_LORA, Q_LORA + KV_LORA, Q_LORA + KV_LORA + D_B, Q_LORA + KV_LORA + 2 * D_B

    z = []
    for r in blocks:
        h = _rms(x_ref[0, r], gpre_ref[...]) * (1.0 + scale) + shift
        z.append(_dot(h.astype(BF16), win_ref[...]))

    lo0, hi0, end = QK_NOPE, QK_NOPE + ROPE_HALF, QK_NOPE + QK_ROPE
    qt = [_dot_nt(wqt_ref[...], _rms(zi[:, 0:Q_LORA], gq_ref[...]).astype(BF16)) for zi in z]
    kvn = [_rms(zi[:, o_kv:o_u], gkv_ref[...]).astype(BF16) for zi in z]
    kk = [_dot(v, wk_ref[...]) for v in kvn]
    vt = [_dot_nt(wvt_ref[...], v) for v in kvn]
    ones_lane = (lax.broadcasted_iota(jnp.int32, (1, SLAB), 1) == STAB_ROW).astype(F32)
    ones_row = (lax.broadcasted_iota(jnp.int32, (V_SLAB, 1), 0) == V_DIM).astype(F32)
    for r, zi, qti, kki, vti in zip(blocks, z, qt, kk, vt):
        cos_t, sin_t = ropeq_ref[0, :, r], ropeq_ref[1, :, r]
        for hd in range(N_HEADS):
            base = hd * SLAB
            x_lo = qti[base + lo0:base + hi0]
            x_hi = qti[base + hi0:base + end]
            slab = jnp.concatenate([
                qti[base:base + lo0] * Q_SCALE,
                (x_lo * cos_t - x_hi * sin_t) * Q_SCALE,
                (x_lo * sin_t + x_hi * cos_t) * Q_SCALE,
                qti[base + end:base + SLAB],
            ], axis=0)
            qt_out[0, base:base + SLAB, r] = slab.astype(BF16)
        kr = _rope_slab(zi[:, o_kr:o_kr + SLAB], ropek_ref[0, r], ropek_ref[1, r], ropek_ref[2, r]) + ones_lane
        for hd in range(N_HEADS):
            sl = slice(hd * SLAB, (hd + 1) * SLAB)
            k_out[0, r, sl] = (kki[:, sl] + kr).astype(BF16)
        for hd in range(N_HEADS):
            base = hd * V_SLAB
            vt_out[0, base:base + V_SLAB, r] = (vti[base:base + V_SLAB] + ones_row).astype(BF16)

    low = lax.broadcasted_iota(jnp.int32, (CHUNK, SLAB), 1) < C_B
    for r, zi in zip(blocks, z):
        gu = jax.nn.gelu(zi[:, o_u:o_v])
        vn = _rms(jax.nn.gelu(zi[:, o_v:o_kr]), gsgu_ref[...])
        n_chunks = (r.stop - r.start) // CHUNK
        pair = 2 if n_chunks % 2 == 0 else 1
        for c in range(0, n_chunks, pair):
            for j in range(D_B // SLAB):
                cols = slice(j * SLAB, (j + 1) * SLAB)
                rhs = []
                for cc in range(c, c + pair):
                    slab = vn[cc * CHUNK:(cc + 1) * CHUNK, cols]
                    rhs.append(jnp.concatenate([jnp.where(low, slab, 0.0), jnp.where(low, 0.0, slab)], axis=0))
                mixed = _dot(ws_ref[j], jnp.concatenate(rhs, axis=1).astype(BF16))
                for k, cc in enumerate(range(c, c + pair)):
                    out_rows = slice(r.start + cc * CHUNK, r.start + (cc + 1) * CHUNK)
                    mixed_sc[out_rows, cols] = mixed[:, k * SLAB:(k + 1) * SLAB] + bs_ref[:, cols]
        sgu = gu * mixed_sc[r]
        sg_out[0, r] = _rms(sgu, goutb_ref[...]).astype(BF16)


def _layer_spec(shape, l, **kw):
    return pl.BlockSpec((None,) + tuple(shape), lambda bi, i: (l,) + (0,) * len(shape), **kw)


def _mod_spec(l, row0):
    return pl.BlockSpec((None, 1, 1, N_MOD * D_MODEL), lambda bi, i: (l, row0 + bi, 0, 0))


def _premix(x, mod, p, l, row0, rope_k, rope_q):
    b, s, _ = x.shape
    t = _token_tile(s, 1024)
    tok = lambda bi, i: (bi, i, 0)
    tok_t = lambda bi, i: (bi, 0, i)
    n_slab = N_HEADS * SLAB
    layer = functools.partial(_layer_spec, l=l)
    return pl.pallas_call(
        _premix_kernel,
        out_shape=(
            jax.ShapeDtypeStruct((b, n_slab, s), BF16),
            jax.ShapeDtypeStruct((b, s, n_slab), BF16),
            jax.ShapeDtypeStruct((b, N_HEADS * V_SLAB, s), BF16),
            jax.ShapeDtypeStruct((b, s, D_B), BF16),
        ),
        grid=(b, s // t),
        in_specs=[
            pl.BlockSpec((1, t, D_MODEL), tok),
            _mod_spec(l, row0),
            layer((1, D_MODEL)),
            layer((D_MODEL, IN_COLS_PADDED)),
            layer((1, Q_LORA)),
            layer((n_slab, Q_LORA)),
            layer((1, KV_LORA)),
            layer((KV_LORA, n_slab)),
            layer((N_HEADS * V_SLAB, KV_LORA)),
            layer((1, D_B)),
            layer((D_B // SLAB, CHUNK, 2 * CHUNK)),
            layer((CHUNK, D_B)),
            layer((1, D_B)),
            pl.BlockSpec((3, t, SLAB), lambda bi, i: (0, i, 0)),
            pl.BlockSpec((2, ROPE_HALF, t), lambda bi, i: (0, 0, i)),
        ],
        out_specs=(
            pl.BlockSpec((1, n_slab, t), tok_t),
            pl.BlockSpec((1, t, n_slab), tok),
            pl.BlockSpec((1, N_HEADS * V_SLAB, t), tok_t),
            pl.BlockSpec((1, t, D_B), tok),
        ),
        scratch_shapes=[pltpu.VMEM((t, D_B), F32)],
        compiler_params=pltpu.CompilerParams(
            dimension_semantics=("arbitrary", "arbitrary"), vmem_limit_bytes=VMEM_LIMIT),
        name="premix",
    )(x, mod, p["g_pre_mix"], p["w_in"], p["g_q_a"], p["w_qt"], p["g_kv_a"], p["w_k"], p["w_vt"], p["g_sgu"],
      p["w_spatial"], p["b_spatial"], p["g_out_b"], rope_k, rope_q)


def _round_up_bf16(x):
    return (x + jnp.abs(x) * (1.0 / 128.0)).astype(BF16).astype(F32)


def _attn_kernel(qt_ref, k_ref, vt_ref, o_ref, qs_sc, m_sc, acc_sc, p_sc, *, tk):
    n_chunks = k_ref.shape[1] // tk
    tq = qs_sc.shape[-1]
    heads = range(HEADS_PER_STEP)
    pad_rows = jnp.zeros((STAB_ROWS - 1, tq), BF16)

    def set_stabiliser(hd, m):
        qs_sc[hd, STAB_ROW:STAB_ROW + STAB_ROWS, :] = jnp.concatenate([(-m).astype(BF16), pad_rows], axis=0)

    def chunk(c, src, dst):
        start = pl.multiple_of(c * tk, tk)
        floor = jnp.where(c == 0, -GROWTH_LIMIT, NEG_BIG)
        hi = lo = None
        for hd in heads:
            sl = slice(hd * SLAB, (hd + 1) * SLAB)
            m_old = m_sc[src, hd]
            st = _dot(k_ref[0, pl.ds(start, tk), sl], qs_sc[hd])
            mx = jnp.max(st, axis=0, keepdims=True)
            vt = vt_ref[0, hd * V_SLAB:(hd + 1) * V_SLAB, pl.ds(start, tk)]
            p_sc[hd % 2] = jnp.exp2(st.astype(BF16))
            pv = _dot(vt, p_sc[hd % 2])
            m_new = jnp.where((mx > 0.0) | (mx < floor), _round_up_bf16(m_old + mx), m_old)
            acc_sc[dst, hd] = (acc_sc[src, hd] + pv) * jnp.exp2(jnp.minimum(m_old - m_new, 0.0))
            m_sc[dst, hd] = m_new
            top, bot = jnp.max(mx), jnp.min(mx)
            hi = top if hi is None else jnp.maximum(hi, top)
            lo = bot if lo is None else jnp.minimum(lo, bot)
        return hi, lo, floor

    def step(c, src, dst):
        hi, lo, floor = chunk(c, src, dst)

        @pl.when((hi > GROWTH_LIMIT) | (lo < floor))
        def _():
            for hd in heads:
                m_new = m_sc[dst, hd]
                acc_sc[src, hd] = acc_sc[src, hd] * jnp.exp2(jnp.minimum(m_sc[src, hd] - m_new, 0.0))
                m_sc[src, hd] = m_new
                set_stabiliser(hd, m_new)
            chunk(c, src, dst)

        for hd in heads:
            set_stabiliser(hd, m_sc[dst, hd])

    for hd in heads:
        qs_sc[hd] = qt_ref[0, hd * SLAB:(hd + 1) * SLAB, :]
        m_sc[0, hd] = jnp.zeros(m_sc.shape[2:], F32)
        acc_sc[0, hd] = jnp.zeros(acc_sc.shape[2:], F32)

    def body(c, carry):
        for par in range(2):
            @pl.when(c % 2 == par)
            def _():
                step(c, par, 1 - par)
        return carry

    lax.fori_loop(0, n_chunks, body, 0)
    last = n_chunks % 2
    outs = []
    for hd in heads:
        acc = acc_sc[last, hd]
        outs.append(acc[0:V_DIM] / acc[V_DIM:V_DIM + 1])
    o_ref[0] = jnp.concatenate(outs, axis=0).T.astype(o_ref.dtype)


def _attention(qt, k, vt):
    b, s, _ = k.shape
    tq = min(s, Q_TILE)
    tk = min(s, KV_CHUNK)
    assert s % tq == 0 and s % tk == 0 and tk % LANES == 0
    w = HEADS_PER_STEP * SLAB
    return pl.pallas_call(
        functools.partial(_attn_kernel, tk=tk),
        out_shape=jax.ShapeDtypeStruct((b, s, D_A), BF16),
        grid=(b, N_HEADS // HEADS_PER_STEP, s // tq),
        in_specs=[
            pl.BlockSpec((1, w, tq), lambda bi, hp, i: (bi, hp, i)),
            pl.BlockSpec((1, s, w), lambda bi, hp, i: (bi, 0, hp)),
            pl.BlockSpec((1, HEADS_PER_STEP * V_SLAB, s), lambda bi, hp, i: (bi, hp, 0)),
        ],
        out_specs=pl.BlockSpec((1, tq, HEADS_PER_STEP * V_DIM), lambda bi, hp, i: (bi, i, hp)),
        scratch_shapes=[
            pltpu.VMEM((HEADS_PER_STEP, SLAB, tq), BF16),
            pltpu.VMEM((2, HEADS_PER_STEP, 1, tq), F32),
            pltpu.VMEM((2, HEADS_PER_STEP, V_SLAB, tq), F32),
            pltpu.VMEM((2, tk, tq), BF16),
        ],
        compiler_params=pltpu.CompilerParams(
            dimension_semantics=("arbitrary", "arbitrary", "arbitrary"), vmem_limit_bytes=VMEM_LIMIT,
        ),
        name="mla_attention",
    )(qt, k, vt)


def _postmix_kernel(x_ref, mod_ref, attn_ref, sg_ref, gouta_ref, wout_ref, gpost_ref, gpre_ref, wgate_ref,
                    wup_ref, wdown_ref, gpostf_ref, o_ref):
    gate1 = mod_ref[0, :, 2 * D_MODEL:3 * D_MODEL]
    shift2 = mod_ref[0, :, 3 * D_MODEL:4 * D_MODEL]
    scale2 = mod_ref[0, :, 4 * D_MODEL:5 * D_MODEL]
    gate2 = mod_ref[0, :, 5 * D_MODEL:6 * D_MODEL]
    t = x_ref.shape[1]
    n_sub = t // SUBTILE_ROWS if t % SUBTILE_ROWS == 0 else 1
    blocks = [slice(i * t // n_sub, (i + 1) * t // n_sub) for i in range(n_sub)]
    merged = [jnp.concatenate([_rms(attn_ref[0, r].astype(F32), gouta_ref[...]).astype(BF16), sg_ref[0, r]], axis=1)
              for r in blocks]
    m = [_dot(v, wout_ref[...]) for v in merged]
    x1 = [x_ref[0, r] + gate1 * _rms(v, gpost_ref[...]) for r, v in zip(blocks, m)]
    h = [(_rms(v, gpre_ref[...]) * (1.0 + scale2) + shift2).astype(BF16) for v in x1]
    f = [None] * n_sub
    for lo, hi in zip(FF_PIECES[:-1], FF_PIECES[1:]):
        cols = slice(lo, hi)
        for i in range(n_sub):
            act = (jax.nn.silu(_dot(h[i], wgate_ref[:, cols])) * _dot(h[i], wup_ref[:, cols])).astype(BF16)
            part = _dot(act, wdown_ref[cols, :])
            f[i] = part if f[i] is None else f[i] + part
    for r, v1, vf in zip(blocks, x1, f):
        o_ref[0, r] = v1 + gate2 * _rms(vf, gpostf_ref[...])


def _postmix(x, mod, attn, sg, p, l, row0):
    b, s, _ = x.shape
    t = _token_tile(s, 1024)
    tok = lambda bi, i: (bi, i, 0)
    layer = functools.partial(_layer_spec, l=l)
    resident = functools.partial(_layer_spec, l=l, pipeline_mode=pl.Buffered(1))
    return pl.pallas_call(
        _postmix_kernel,
        out_shape=jax.ShapeDtypeStruct(x.shape, F32),
        grid=(b, s // t),
        in_specs=[
            pl.BlockSpec((1, t, D_MODEL), tok),
            _mod_spec(l, row0),
            pl.BlockSpec((1, t, D_A), tok),
            pl.BlockSpec((1, t, D_B), tok),
            layer((1, D_A)),
            resident((D_A + D_B, D_MODEL)),
            layer((1, D_MODEL)),
            layer((1, D_MODEL)),
            resident((D_MODEL, D_FF)),
            resident((D_MODEL, D_FF)),
            resident((D_FF, D_MODEL)),
            layer((1, D_MODEL)),
        ],
        out_specs=pl.BlockSpec((1, t, D_MODEL), tok),
        compiler_params=pltpu.CompilerParams(
            dimension_semantics=("arbitrary", "arbitrary"), vmem_limit_bytes=VMEM_LIMIT),
        name="postmix",
    )(x, mod, attn, sg, p["g_out_a"], p["w_out"], p["g_post_mix"], p["g_pre_ffn"], p["w_gate"], p["w_up"],
      p["w_down"], p["g_post_ffn"])


def _pad_heads(w, width, slab=SLAB):
    lead = w.shape[:-1]
    w = w.reshape(lead + (N_HEADS, width))
    return jnp.pad(w, ((0, 0),) * (len(lead) + 1) + ((0, slab - width),)).reshape(lead + (N_HEADS * slab,))


def _prep_params(w_in, g_q_a, w_q_b, g_kv_a, w_kv_b, g_sgu, w_spatial, b_spatial, g_out_a, g_out_b, w_out,
                 g_pre_mix, g_post_mix, g_pre_ffn, g_post_ffn, w_gate, w_up, w_down):
    o_kr = Q_LORA + KV_LORA
    o_u = o_kr + QK_ROPE
    kr_slab = jnp.pad(w_in[:, :, o_kr:o_u], ((0, 0), (0, 0), (QK_NOPE, SLAB - QK_NOPE - QK_ROPE)))
    w_in_p = jnp.concatenate([w_in[:, :, :o_kr], w_in[:, :, o_u:], kr_slab], axis=2)
    kvw = w_kv_b.reshape(DEPTH, KV_LORA, N_HEADS, QK_NOPE + V_DIM)
    w_k = _pad_heads(kvw[..., :QK_NOPE].reshape(DEPTH, KV_LORA, N_HEADS * QK_NOPE), QK_NOPE)
    w_v = _pad_heads(kvw[..., QK_NOPE:].reshape(DEPTH, KV_LORA, N_HEADS * V_DIM), V_DIM, V_SLAB)
    ws_pairs = jnp.concatenate([w_spatial[:, 0::2], w_spatial[:, 1::2]], axis=3)
    bias = jnp.repeat(jnp.swapaxes(b_spatial, 1, 2), C_B, axis=2)
    row = lambda g: g.reshape(DEPTH, 1, -1)
    t_bf16 = lambda w: jnp.swapaxes(w, 1, 2).astype(BF16)
    return dict(
        g_pre_mix=row(g_pre_mix), w_in=w_in_p.astype(BF16), g_q_a=row(g_q_a),
        w_qt=t_bf16(_pad_heads(w_q_b, QK_NOPE + QK_ROPE)), g_kv_a=row(g_kv_a),
        w_k=w_k.astype(BF16), w_vt=t_bf16(w_v), g_sgu=row(g_sgu),
        w_spatial=ws_pairs.astype(BF16), b_spatial=bias, g_out_b=row(g_out_b), g_out_a=row(g_out_a),
        w_out=w_out.astype(BF16), g_post_mix=row(g_post_mix), g_pre_ffn=row(g_pre_ffn),
        w_gate=w_gate.astype(BF16), w_up=w_up.astype(BF16), w_down=w_down.astype(BF16),
        g_post_ffn=row(g_post_ffn))


def _rope_tables(seq):
    pos = jnp.arange(seq, dtype=F32)
    inv_freq = ROPE_THETA ** (-jnp.arange(0, QK_ROPE, 2, dtype=F32) / QK_ROPE)
    ang = pos[:, None] * inv_freq[None, :]
    cos, sin = jnp.cos(ang), jnp.sin(ang)
    zeros = lambda n: jnp.zeros((seq, n), F32)
    tail = SLAB - QK_NOPE - QK_ROPE
    c = jnp.concatenate([jnp.ones((seq, QK_NOPE), F32), cos, cos, zeros(tail)], axis=1)
    s_lo = jnp.concatenate([zeros(QK_NOPE), -sin, zeros(ROPE_HALF + tail)], axis=1)
    s_hi = jnp.concatenate([zeros(QK_NOPE + ROPE_HALF), sin, zeros(tail)], axis=1)
    return jnp.stack([c, s_lo, s_hi]), jnp.stack([cos.T, sin.T])


def _trunk(x, mod, row0, params, rope_k, rope_q):
    for l in range(DEPTH):
        qt, k, vt, sg = _premix(x, mod, params, l, row0, rope_k, rope_q)
        attn = _attention(qt, k, vt)
        x = _postmix(x, mod, attn, sg, params, l, row0)
    return x


def kernel(x_prompt, x_sample, c_prompt, c_sample, w_mod, b_mod, g_pre_mix, g_post_mix, g_pre_ffn, g_post_ffn,
           w_in, g_q_a, w_q_b, g_kv_a, w_kv_b, g_sgu, w_spatial, b_spatial, g_out_a, g_out_b, w_out,
           w_gate, w_up, w_down):
    params = _prep_params(w_in, g_q_a, w_q_b, g_kv_a, w_kv_b, g_sgu, w_spatial, b_spatial, g_out_a, g_out_b, w_out,
                          g_pre_mix, g_post_mix, g_pre_ffn, g_post_ffn, w_gate, w_up, w_down)
    nb_p, nb_s = c_prompt.shape[0], c_sample.shape[0]
    rows = -(-(nb_p + nb_s) // 8) * 8
    c_all = jnp.concatenate([c_prompt, c_sample, jnp.zeros((rows - nb_p - nb_s, D_MODEL), F32)], axis=0)
    mod = _modulation(c_all, w_mod, b_mod).reshape(DEPTH, rows, 1, N_MOD * D_MODEL)
    rope_k, rope_q = _rope_tables(max(x_prompt.shape[1], x_sample.shape[1]))
    y_prompt = _trunk(x_prompt, mod, 0, params, rope_k, rope_q)
    y_sample = _trunk(x_sample, mod, nb_p, params, rope_k, rope_q)
    return (y_prompt, y_sample)
```

```python
import functools
import math

import jax
import jax.numpy as jnp
from jax import lax
from jax.experimental import pallas as pl
from jax.experimental.pallas import tpu as pltpu

D_MODEL = 1024
DEPTH = 2
N_HEADS = 8
QK_NOPE = 64
QK_ROPE = 32
ROPE_HALF = QK_ROPE // 2
V_DIM = 64
Q_LORA = 384
KV_LORA = 256
D_A = N_HEADS * V_DIM
D_B = 512
G_B = 8
C_B = D_B // G_B
CHUNK = 128
D_FF = 2816
EPS = 1e-6
N_MOD = 6
ROPE_THETA = 10000.0

LANES = 128
SLAB = LANES
V_SLAB = 80
HEADS_PER_STEP = 4
Q_TILE = 1024
KV_CHUNK = 2048
STAB_ROW = QK_NOPE + QK_ROPE
STAB_PARTS = 2
STAB_ROWS = 16
GROWTH_LIMIT = 64.0
IN_COLS_PADDED = Q_LORA + KV_LORA + 2 * D_B + SLAB
SUBTILE_ROWS = 256
MXU_DEPTH = 256
FF_PIECES = (0, 6 * MXU_DEPTH, D_FF)
VMEM_LIMIT = 56 * 1024 * 1024

BF16 = jnp.bfloat16
F32 = jnp.float32
NEG_BIG = -1e30
Q_SCALE = math.log2(math.e) / math.sqrt(QK_NOPE + QK_ROPE)


def _token_tile(seq, tile=512):
    return min(seq, tile)


def _rms(x, g):
    r = lax.rsqrt(jnp.mean(x * x, axis=-1, keepdims=True) + EPS)
    return x * r * g


def _dot(a, b):
    return jnp.dot(a, b, preferred_element_type=F32)


def _dot_nt(a, b):
    return lax.dot_general(a, b, (((1,), (1,)), ((), ())), preferred_element_type=F32)


def _mod_kernel(c_ref, w_ref, b_ref, o_ref):
    cs = jax.nn.silu(c_ref[...])
    o_ref[0] = _dot(cs.astype(BF16), w_ref[0].astype(BF16)) + b_ref[0]


def _modulation(c_all, w_mod, b_mod):
    rows = c_all.shape[0]
    n = w_mod.shape[-1]
    tn = 1536
    return pl.pallas_call(
        _mod_kernel,
        out_shape=jax.ShapeDtypeStruct((DEPTH, rows, n), F32),
        grid=(DEPTH, n // tn),
        in_specs=[
            pl.BlockSpec((rows, D_MODEL), lambda l, j: (0, 0)),
            pl.BlockSpec((1, D_MODEL, tn), lambda l, j: (l, 0, j)),
            pl.BlockSpec((1, 1, tn), lambda l, j: (l, 0, j)),
        ],
        out_specs=pl.BlockSpec((1, rows, tn), lambda l, j: (l, 0, j)),
        compiler_params=pltpu.CompilerParams(
            dimension_semantics=("arbitrary", "arbitrary"), vmem_limit_bytes=VMEM_LIMIT),
        name="adaln_modulation",
    )(c_all, w_mod, b_mod.reshape(DEPTH, 1, n))


def _rope_slab(x, c, s_lo, s_hi):
    return x * c + pltpu.roll(x, SLAB - ROPE_HALF, 1) * s_lo + pltpu.roll(x, ROPE_HALF, 1) * s_hi


def _premix_kernel(x_ref, mod_ref, gpre_ref, win_ref, gq_ref, wqt_ref, gkv_ref, wk_ref, wvt_ref, gsgu_ref, ws_ref,
                   bs_ref, goutb_ref, ropek_ref, ropeq_ref, qt_out, k_out, vt_out, sg_out, mixed_sc):
    shift = mod_ref[0, :, 0:D_MODEL]
    scale = mod_ref[0, :, D_MODEL:2 * D_MODEL]
    t = x_ref.shape[1]
    n_sub = t // SUBTILE_ROWS if t % SUBTILE_ROWS == 0 else 1
    blocks = [slice(i * t // n_sub, (i + 1) * t // n_sub) for i in range(n_sub)]
    o_kv, o_u, o_v, o_kr = Q_LORA, Q_LORA + KV_LORA, Q_LORA + KV_LORA + D_B, Q_LORA + KV_LORA + 2 * D_B

    z = []
    for r in blocks:
        h = _rms(x_ref[0, r], gpre_ref[...]) * (1.0 + scale) + shift
        z.append(_dot(h.astype(BF16), win_ref[...]))

    lo0, hi0, end = QK_NOPE, QK_NOPE + ROPE_HALF, QK_NOPE + QK_ROPE
    qt = [_dot_nt(wqt_ref[...], _rms(zi[:, 0:Q_LORA], gq_ref[...]).astype(BF16)) for zi in z]
    kvn = [_rms(zi[:, o_kv:o_u], gkv_ref[...]).astype(BF16) for zi in z]
    kk = [_dot(v, wk_ref[...]) for v in kvn]
    vt = [_dot_nt(wvt_ref[...], v) for v in kvn]
    lane = lax.broadcasted_iota(jnp.int32, (1, SLAB), 1)
    ones_lane = ((lane >= STAB_ROW) & (lane < STAB_ROW + STAB_PARTS)).astype(F32)
    ones_row = (lax.broadcasted_iota(jnp.int32, (V_SLAB, 1), 0) == V_DIM).astype(F32)
    for r, zi, qti, kki, vti in zip(blocks, z, qt, kk, vt):
        cos_t, sin_t = ropeq_ref[0, :, r], ropeq_ref[1, :, r]
        for hd in range(N_HEADS):
            base = hd * SLAB
            x_lo = qti[base + lo0:base + hi0]
            x_hi = qti[base + hi0:base + end]
            slab = jnp.concatenate([
                qti[base:base + lo0] * Q_SCALE,
                (x_lo * cos_t - x_hi * sin_t) * Q_SCALE,
                (x_lo * sin_t + x_hi * cos_t) * Q_SCALE,
                qti[base + end:base + SLAB],
            ], axis=0)
            qt_out[0, base:base + SLAB, r] = slab.astype(BF16)
        kr = _rope_slab(zi[:, o_kr:o_kr + SLAB], ropek_ref[0, r], ropek_ref[1, r], ropek_ref[2, r]) + ones_lane
        for hd in range(N_HEADS):
            sl = slice(hd * SLAB, (hd + 1) * SLAB)
            k_out[0, r, sl] = (kki[:, sl] + kr).astype(BF16)
        for hd in range(N_HEADS):
            base = hd * V_SLAB
            vt_out[0, base:base + V_SLAB, r] = (vti[base:base + V_SLAB] + ones_row).astype(BF16)

    low = lax.broadcasted_iota(jnp.int32, (CHUNK, SLAB), 1) < C_B
    for r, zi in zip(blocks, z):
        gu = jax.nn.gelu(zi[:, o_u:o_v])
        vn = _rms(jax.nn.gelu(zi[:, o_v:o_kr]), gsgu_ref[...])
        n_chunks = (r.stop - r.start) // CHUNK
        pair = 2 if n_chunks % 2 == 0 else 1
        for c in range(0, n_chunks, pair):
            for j in range(D_B // SLAB):
                cols = slice(j * SLAB, (j + 1) * SLAB)
                rhs = []
                for cc in range(c, c + pair):
                    slab = vn[cc * CHUNK:(cc + 1) * CHUNK, cols]
                    rhs.append(jnp.concatenate([jnp.where(low, slab, 0.0), jnp.where(low, 0.0, slab)], axis=0))
                mixed = _dot(ws_ref[j], jnp.concatenate(rhs, axis=1).astype(BF16))
                for k, cc in enumerate(range(c, c + pair)):
                    out_rows = slice(r.start + cc * CHUNK, r.start + (cc + 1) * CHUNK)
                    mixed_sc[out_rows, cols] = mixed[:, k * SLAB:(k + 1) * SLAB] + bs_ref[:, cols]
        sgu = gu * mixed_sc[r]
        sg_out[0, r] = _rms(sgu, goutb_ref[...]).astype(BF16)


def _layer_spec(shape, l, **kw):
    return pl.BlockSpec((None,) + tuple(shape), lambda bi, i: (l,) + (0,) * len(shape), **kw)


def _mod_spec(l, row0):
    return pl.BlockSpec((None, 1, 1, N_MOD * D_MODEL), lambda bi, i: (l, row0 + bi, 0, 0))


def _premix(x, mod, p, l, row0, rope_k, rope_q):
    b, s, _ = x.shape
    t = _token_tile(s, 1024)
    tok = lambda bi, i: (bi, i, 0)
    tok_t = lambda bi, i: (bi, 0, i)
    n_slab = N_HEADS * SLAB
    layer = functools.partial(_layer_spec, l=l)
    return pl.pallas_call(
        _premix_kernel,
        out_shape=(
            jax.ShapeDtypeStruct((b, n_slab, s), BF16),
            jax.ShapeDtypeStruct((b, s, n_slab), BF16),
            jax.ShapeDtypeStruct((b, N_HEADS * V_SLAB, s), BF16),
            jax.ShapeDtypeStruct((b, s, D_B), BF16),
        ),
        grid=(b, s // t),
        in_specs=[
            pl.BlockSpec((1, t, D_MODEL), tok),
            _mod_spec(l, row0),
            layer((1, D_MODEL)),
            layer((D_MODEL, IN_COLS_PADDED)),
            layer((1, Q_LORA)),
            layer((n_slab, Q_LORA)),
            layer((1, KV_LORA)),
            layer((KV_LORA, n_slab)),
            layer((N_HEADS * V_SLAB, KV_LORA)),
            layer((1, D_B)),
            layer((D_B // SLAB, CHUNK, 2 * CHUNK)),
            layer((CHUNK, D_B)),
            layer((1, D_B)),
            pl.BlockSpec((3, t, SLAB), lambda bi, i: (0, i, 0)),
            pl.BlockSpec((2, ROPE_HALF, t), lambda bi, i: (0, 0, i)),
        ],
        out_specs=(
            pl.BlockSpec((1, n_slab, t), tok_t),
            pl.BlockSpec((1, t, n_slab), tok),
            pl.BlockSpec((1, N_HEADS * V_SLAB, t), tok_t),
            pl.BlockSpec((1, t, D_B), tok),
        ),
        scratch_shapes=[pltpu.VMEM((t, D_B), F32)],
        compiler_params=pltpu.CompilerParams(
            dimension_semantics=("arbitrary", "arbitrary"), vmem_limit_bytes=VMEM_LIMIT),
        name="premix",
    )(x, mod, p["g_pre_mix"], p["w_in"], p["g_q_a"], p["w_qt"], p["g_kv_a"], p["w_k"], p["w_vt"], p["g_sgu"],
      p["w_spatial"], p["b_spatial"], p["g_out_b"], rope_k, rope_q)


def _two_bf16(x):
    xn = x + jnp.abs(x) * 2.0 ** -15
    hi = xn.astype(BF16).astype(F32)
    lo = (xn - hi).astype(BF16).astype(F32)
    return jnp.concatenate([hi, lo], axis=0)


def _attn_kernel(qt_ref, k_ref, vt_ref, o_ref, qs_sc, m_sc, acc_sc, p_sc, *, tk):
    n_chunks = k_ref.shape[1] // tk
    tq = qs_sc.shape[-1]
    heads = range(HEADS_PER_STEP)
    pad_rows = jnp.zeros((STAB_ROWS - STAB_PARTS, tq), BF16)

    def set_stabiliser(hd, parts):
        qs_sc[hd, STAB_ROW:STAB_ROW + STAB_ROWS, :] = jnp.concatenate([(-parts).astype(BF16), pad_rows], axis=0)

    def chunk(c, src, dst, floor):
        start = pl.multiple_of(c * tk, tk)
        hi = lo = None
        for hd in heads:
            sl = slice(hd * SLAB, (hd + 1) * SLAB)
            parts = m_sc[src, hd]
            m_old = parts[0:1] + parts[1:2]
            st = _dot(k_ref[0, pl.ds(start, tk), sl], qs_sc[hd])
            mx = jnp.max(st, axis=0, keepdims=True)
            vt = vt_ref[0, hd * V_SLAB:(hd + 1) * V_SLAB, pl.ds(start, tk)]
            p_sc[hd % 2] = jnp.exp2(st).astype(BF16)
            pv = _dot(vt, p_sc[hd % 2])
            parts = jnp.where((mx > 0.0) | (mx < floor), _two_bf16(m_old + mx), parts)
            m_new = parts[0:1] + parts[1:2]
            acc_sc[dst, hd] = (acc_sc[src, hd] + pv) * jnp.exp2(jnp.minimum(m_old - m_new, 0.0))
            m_sc[dst, hd] = parts
            top, bot = jnp.max(mx), jnp.min(mx)
            hi = top if hi is None else jnp.maximum(hi, top)
            lo = bot if lo is None else jnp.minimum(lo, bot)
        return hi, lo

    def step(c, src, dst):
        floor = jnp.where(c == 0, -GROWTH_LIMIT, NEG_BIG)
        hi, lo = chunk(c, src, dst, floor)

        @pl.when((hi > GROWTH_LIMIT) | (lo < floor))
        def _():
            for hd in heads:
                new, old = m_sc[dst, hd], m_sc[src, hd]
                shift = (old[0:1] + old[1:2]) - (new[0:1] + new[1:2])
                acc_sc[src, hd] = acc_sc[src, hd] * jnp.exp2(jnp.minimum(shift, 0.0))
                m_sc[src, hd] = new
                set_stabiliser(hd, new)
            chunk(c, src, dst, NEG_BIG)

        for hd in heads:
            set_stabiliser(hd, m_sc[dst, hd])

    for hd in heads:
        qs_sc[hd] = qt_ref[0, hd * SLAB:(hd + 1) * SLAB, :]
        m_sc[0, hd] = jnp.zeros(m_sc.shape[2:], F32)
        acc_sc[0, hd] = jnp.zeros(acc_sc.shape[2:], F32)

    def body(c, carry):
        for par in range(2):
            @pl.when(c % 2 == par)
            def _():
                step(c, par, 1 - par)
        return carry

    lax.fori_loop(0, n_chunks, body, 0)
    last = n_chunks % 2
    outs = []
    for hd in heads:
        acc = acc_sc[last, hd]
        outs.append(acc[0:V_DIM] / acc[V_DIM:V_DIM + 1])
    o_ref[0] = jnp.concatenate(outs, axis=0).T.astype(o_ref.dtype)


def _attention(qt, k, vt):
    b, s, _ = k.shape
    tq = min(s, Q_TILE)
    tk = min(s, KV_CHUNK)
    assert s % tq == 0 and s % tk == 0 and tk % LANES == 0
    w = HEADS_PER_STEP * SLAB
    return pl.pallas_call(
        functools.partial(_attn_kernel, tk=tk),
        out_shape=jax.ShapeDtypeStruct((b, s, D_A), BF16),
        grid=(b, N_HEADS // HEADS_PER_STEP, s // tq),
        in_specs=[
            pl.BlockSpec((1, w, tq), lambda bi, hp, i: (bi, hp, i)),
            pl.BlockSpec((1, s, w), lambda bi, hp, i: (bi, 0, hp)),
            pl.BlockSpec((1, HEADS_PER_STEP * V_SLAB, s), lambda bi, hp, i: (bi, hp, 0)),
        ],
        out_specs=pl.BlockSpec((1, tq, HEADS_PER_STEP * V_DIM), lambda bi, hp, i: (bi, i, hp)),
        scratch_shapes=[
            pltpu.VMEM((HEADS_PER_STEP, SLAB, tq), BF16),
            pltpu.VMEM((2, HEADS_PER_STEP, STAB_PARTS, tq), F32),
            pltpu.VMEM((2, HEADS_PER_STEP, V_SLAB, tq), F32),
            pltpu.VMEM((2, tk, tq), BF16),
        ],
        compiler_params=pltpu.CompilerParams(
            dimension_semantics=("arbitrary", "arbitrary", "arbitrary"), vmem_limit_bytes=VMEM_LIMIT,
        ),
        name="mla_attention",
    )(qt, k, vt)


def _postmix_kernel(x_ref, mod_ref, attn_ref, sg_ref, gouta_ref, wout_ref, gpost_ref, gpre_ref, wgate_ref,
                    wup_ref, wdown_ref, gpostf_ref, o_ref):
    gate1 = mod_ref[0, :, 2 * D_MODEL:3 * D_MODEL]
    shift2 = mod_ref[0, :, 3 * D_MODEL:4 * D_MODEL]
    scale2 = mod_ref[0, :, 4 * D_MODEL:5 * D_MODEL]
    gate2 = mod_ref[0, :, 5 * D_MODEL:6 * D_MODEL]
    t = x_ref.shape[1]
    n_sub = t // SUBTILE_ROWS if t % SUBTILE_ROWS == 0 else 1
    blocks = [slice(i * t // n_sub, (i + 1) * t // n_sub) for i in range(n_sub)]
    merged = [jnp.concatenate([_rms(attn_ref[0, r].astype(F32), gouta_ref[...]).astype(BF16), sg_ref[0, r]], axis=1)
              for r in blocks]
    m = [_dot(v, wout_ref[...]) for v in merged]
    x1 = [x_ref[0, r] + gate1 * _rms(v, gpost_ref[...]) for r, v in zip(blocks, m)]
    h = [(_rms(v, gpre_ref[...]) * (1.0 + scale2) + shift2).astype(BF16) for v in x1]
    f = [None] * n_sub
    for lo, hi in zip(FF_PIECES[:-1], FF_PIECES[1:]):
        cols = slice(lo, hi)
        for i in range(n_sub):
            act = (jax.nn.silu(_dot(h[i], wgate_ref[:, cols])) * _dot(h[i], wup_ref[:, cols])).astype(BF16)
            part = _dot(act, wdown_ref[cols, :])
            f[i] = part if f[i] is None else f[i] + part
    for r, v1, vf in zip(blocks, x1, f):
        o_ref[0, r] = v1 + gate2 * _rms(vf, gpostf_ref[...])


def _postmix(x, mod, attn, sg, p, l, row0):
    b, s, _ = x.shape
    t = _token_tile(s, 1024)
    tok = lambda bi, i: (bi, i, 0)
    layer = functools.partial(_layer_spec, l=l)
    resident = functools.partial(_layer_spec, l=l, pipeline_mode=pl.Buffered(1))
    return pl.pallas_call(
        _postmix_kernel,
        out_shape=jax.ShapeDtypeStruct(x.shape, F32),
        grid=(b, s // t),
        in_specs=[
            pl.BlockSpec((1, t, D_MODEL), tok),
            _mod_spec(l, row0),
            pl.BlockSpec((1, t, D_A), tok),
            pl.BlockSpec((1, t, D_B), tok),
            layer((1, D_A)),
            resident((D_A + D_B, D_MODEL)),
            layer((1, D_MODEL)),
            layer((1, D_MODEL)),
            resident((D_MODEL, D_FF)),
            resident((D_MODEL, D_FF)),
            resident((D_FF, D_MODEL)),
            layer((1, D_MODEL)),
        ],
        out_specs=pl.BlockSpec((1, t, D_MODEL), tok),
        compiler_params=pltpu.CompilerParams(
            dimension_semantics=("arbitrary", "arbitrary"), vmem_limit_bytes=VMEM_LIMIT),
        name="postmix",
    )(x, mod, attn, sg, p["g_out_a"], p["w_out"], p["g_post_mix"], p["g_pre_ffn"], p["w_gate"], p["w_up"],
      p["w_down"], p["g_post_ffn"])


def _pad_heads(w, width, slab=SLAB):
    lead = w.shape[:-1]
    w = w.reshape(lead + (N_HEADS, width))
    return jnp.pad(w, ((0, 0),) * (len(lead) + 1) + ((0, slab - width),)).reshape(lead + (N_HEADS * slab,))


def _prep_params(w_in, g_q_a, w_q_b, g_kv_a, w_kv_b, g_sgu, w_spatial, b_spatial, g_out_a, g_out_b, w_out,
                 g_pre_mix, g_post_mix, g_pre_ffn, g_post_ffn, w_gate, w_up, w_down):
    w_in, w_q_b, w_kv_b, w_spatial = (w.astype(BF16) for w in (w_in, w_q_b, w_kv_b, w_spatial))
    o_kr = Q_LORA + KV_LORA
    o_u = o_kr + QK_ROPE
    kr_slab = jnp.pad(w_in[:, :, o_kr:o_u], ((0, 0), (0, 0), (QK_NOPE, SLAB - QK_NOPE - QK_ROPE)))
    w_in_p = jnp.concatenate([w_in[:, :, :o_kr], w_in[:, :, o_u:], kr_slab], axis=2)
    kvw = w_kv_b.reshape(DEPTH, KV_LORA, N_HEADS, QK_NOPE + V_DIM)
    w_k = _pad_heads(kvw[..., :QK_NOPE].reshape(DEPTH, KV_LORA, N_HEADS * QK_NOPE), QK_NOPE)
    w_v = _pad_heads(kvw[..., QK_NOPE:].reshape(DEPTH, KV_LORA, N_HEADS * V_DIM), V_DIM, V_SLAB)
    ws_pairs = jnp.concatenate([w_spatial[:, 0::2], w_spatial[:, 1::2]], axis=3)
    bias = jnp.repeat(jnp.swapaxes(b_spatial, 1, 2), C_B, axis=2)
    row = lambda g: g.reshape(DEPTH, 1, -1)
    t_bf16 = lambda w: jnp.swapaxes(w, 1, 2).astype(BF16)
    return dict(
        g_pre_mix=row(g_pre_mix), w_in=w_in_p.astype(BF16), g_q_a=row(g_q_a),
        w_qt=t_bf16(_pad_heads(w_q_b, QK_NOPE + QK_ROPE)), g_kv_a=row(g_kv_a),
        w_k=w_k.astype(BF16), w_vt=t_bf16(w_v), g_sgu=row(g_sgu),
        w_spatial=ws_pairs.astype(BF16), b_spatial=bias, g_out_b=row(g_out_b), g_out_a=row(g_out_a),
        w_out=w_out.astype(BF16), g_post_mix=row(g_post_mix), g_pre_ffn=row(g_pre_ffn),
        w_gate=w_gate.astype(BF16), w_up=w_up.astype(BF16), w_down=w_down.astype(BF16),
        g_post_ffn=row(g_post_ffn))


def _rope_tables(seq):
    pos = jnp.arange(seq, dtype=F32)
    inv_freq = ROPE_THETA ** (-jnp.arange(0, QK_ROPE, 2, dtype=F32) / QK_ROPE)
    ang = pos[:, None] * inv_freq[None, :]
    cos, sin = jnp.cos(ang), jnp.sin(ang)
    zeros = lambda n: jnp.zeros((seq, n), F32)
    tail = SLAB - QK_NOPE - QK_ROPE
    c = jnp.concatenate([jnp.ones((seq, QK_NOPE), F32), cos, cos, zeros(tail)], axis=1)
    s_lo = jnp.concatenate([zeros(QK_NOPE), -sin, zeros(ROPE_HALF + tail)], axis=1)
    s_hi = jnp.concatenate([zeros(QK_NOPE + ROPE_HALF), sin, zeros(tail)], axis=1)
    return jnp.stack([c, s_lo, s_hi]), jnp.stack([cos.T, sin.T])


def _trunk(x, mod, row0, params, rope_k, rope_q):
    for l in range(DEPTH):
        qt, k, vt, sg = _premix(x, mod, params, l, row0, rope_k, rope_q)
        attn = _attention(qt, k, vt)
        x = _postmix(x, mod, attn, sg, params, l, row0)
    return x


def kernel(x_prompt, x_sample, c_prompt, c_sample, w_mod, b_mod, g_pre_mix, g_post_mix, g_pre_ffn, g_post_ffn,
           w_in, g_q_a, w_q_b, g_kv_a, w_kv_b, g_sgu, w_spatial, b_spatial, g_out_a, g_out_b, w_out,
           w_gate, w_up, w_down):
    params = _prep_params(w_in, g_q_a, w_q_b, g_kv_a, w_kv_b, g_sgu, w_spatial, b_spatial, g_out_a, g_out_b, w_out,
                          g_pre_mix, g_post_mix, g_pre_ffn, g_post_ffn, w_gate, w_up, w_down)
    nb_p, nb_s = c_prompt.shape[0], c_sample.shape[0]
    rows = -(-(nb_p + nb_s) // 8) * 8
    c_all = jnp.concatenate([c_prompt, c_sample, jnp.zeros((rows - nb_p - nb_s, D_MODEL), F32)], axis=0)
    mod = _modulation(c_all, w_mod, b_mod).reshape(DEPTH, rows, 1, N_MOD * D_MODEL)
    rope_k, rope_q = _rope_tables(max(x_prompt.shape[1], x_sample.shape[1]))
    y_prompt = _trunk(x_prompt, mod, 0, params, rope_k, rope_q)
    y_sample = _trunk(x_sample, mod, nb_p, params, rope_k, rope_q)
    return (y_prompt, y_sample)
```

```python
import functools
import math

import jax
import jax.numpy as jnp
from jax import lax
from jax.experimental import pallas as pl
from jax.experimental.pallas import tpu as pltpu

D_MODEL = 1024
DEPTH = 2
N_HEADS = 8
QK_NOPE = 64
QK_ROPE = 32
ROPE_HALF = QK_ROPE // 2
V_DIM = 64
Q_LORA = 384
KV_LORA = 256
D_A = N_HEADS * V_DIM
D_B = 512
G_B = 8
C_B = D_B // G_B
CHUNK = 128
D_FF = 2816
EPS = 1e-6
N_MOD = 6
ROPE_THETA = 10000.0

LANES = 128
SLAB = LANES
BF16_ROWS = 16
V_SLAB = V_DIM + BF16_ROWS
HEADS_PER_STEP = 4
Q_TILE = 1024
KV_CHUNK = 2048
STAB_ROW = QK_NOPE + QK_ROPE
STAB_PARTS = 2
STAB_ROWS = BF16_ROWS
GROWTH_LIMIT = 64.0
IN_COLS_PADDED = Q_LORA + KV_LORA + 2 * D_B + SLAB
SUBTILE_ROWS = 256
MXU_DEPTH = 256
FF_PIECES = (0, 6 * MXU_DEPTH, D_FF)
VMEM_LIMIT = 56 * 1024 * 1024

BF16 = jnp.bfloat16
F32 = jnp.float32
NEG_BIG = -1e30
Q_SCALE = math.log2(math.e) / math.sqrt(QK_NOPE + QK_ROPE)


def _token_tile(seq, tile=512):
    return min(seq, tile)


def _rms(x, g):
    r = lax.rsqrt(jnp.mean(x * x, axis=-1, keepdims=True) + EPS)
    return x * r * g


def _dot(a, b):
    return jnp.dot(a, b, preferred_element_type=F32)


def _dot_nt(a, b):
    return lax.dot_general(a, b, (((1,), (1,)), ((), ())), preferred_element_type=F32)


def _mod_kernel(c_ref, w_ref, b_ref, o_ref):
    cs = jax.nn.silu(c_ref[...])
    o_ref[0] = _dot(cs.astype(BF16), w_ref[0].astype(BF16)) + b_ref[0]


def _modulation(c_all, w_mod, b_mod):
    rows = c_all.shape[0]
    n = w_mod.shape[-1]
    tn = 1536
    return pl.pallas_call(
        _mod_kernel,
        out_shape=jax.ShapeDtypeStruct((DEPTH, rows, n), F32),
        grid=(DEPTH, n // tn),
        in_specs=[
            pl.BlockSpec((rows, D_MODEL), lambda l, j: (0, 0)),
            pl.BlockSpec((1, D_MODEL, tn), lambda l, j: (l, 0, j)),
            pl.BlockSpec((1, 1, tn), lambda l, j: (l, 0, j)),
        ],
        out_specs=pl.BlockSpec((1, rows, tn), lambda l, j: (l, 0, j)),
        compiler_params=pltpu.CompilerParams(
            dimension_semantics=("arbitrary", "arbitrary"), vmem_limit_bytes=VMEM_LIMIT),
        name="adaln_modulation",
    )(c_all, w_mod, b_mod.reshape(DEPTH, 1, n))


def _rope_slab(x, c, s_lo, s_hi):
    return x * c + pltpu.roll(x, SLAB - ROPE_HALF, 1) * s_lo + pltpu.roll(x, ROPE_HALF, 1) * s_hi


def _premix_kernel(x_ref, mod_ref, gpre_ref, win_ref, gq_ref, wqt_ref, gkv_ref, wk_ref, wvt_ref, gsgu_ref, ws_ref,
                   bs_ref, goutb_ref, ropek_ref, ropeq_ref, qt_out, k_out, vt_out, sg_out, mixed_sc):
    shift = mod_ref[0, :, 0:D_MODEL]
    scale = mod_ref[0, :, D_MODEL:2 * D_MODEL]
    t = x_ref.shape[1]
    n_sub = t // SUBTILE_ROWS if t % SUBTILE_ROWS == 0 else 1
    blocks = [slice(i * t // n_sub, (i + 1) * t // n_sub) for i in range(n_sub)]
    o_kv, o_u, o_v, o_kr = Q_LORA, Q_LORA + KV_LORA, Q_LORA + KV_LORA + D_B, Q_LORA + KV_LORA + 2 * D_B

    z = []
    for r in blocks:
        h = _rms(x_ref[0, r], gpre_ref[...]) * (1.0 + scale) + shift
        z.append(_dot(h.astype(BF16), win_ref[...]))

    lo0, hi0, end = QK_NOPE, QK_NOPE + ROPE_HALF, QK_NOPE + QK_ROPE
    qt = [_dot_nt(wqt_ref[...], _rms(zi[:, 0:Q_LORA], gq_ref[...]).astype(BF16)) for zi in z]
    kvn = [_rms(zi[:, o_kv:o_u], gkv_ref[...]).astype(BF16) for zi in z]
    kk = [_dot(v, wk_ref[...]) for v in kvn]
    vt = [_dot_nt(wvt_ref[...], v) for v in kvn]
    lane = lax.broadcasted_iota(jnp.int32, (1, SLAB), 1)
    ones_lane = ((lane >= STAB_ROW) & (lane < STAB_ROW + STAB_PARTS)).astype(F32)
    ones_row = (lax.broadcasted_iota(jnp.int32, (V_SLAB, 1), 0) == V_DIM).astype(F32)
    for r, zi, qti, kki, vti in zip(blocks, z, qt, kk, vt):
        cos_t, sin_t = ropeq_ref[0, :, r], ropeq_ref[1, :, r]
        for hd in range(N_HEADS):
            base = hd * SLAB
            x_lo = qti[base + lo0:base + hi0]
            x_hi = qti[base + hi0:base + end]
            slab = jnp.concatenate([
                qti[base:base + lo0] * Q_SCALE,
                (x_lo * cos_t - x_hi * sin_t) * Q_SCALE,
                (x_lo * sin_t + x_hi * cos_t) * Q_SCALE,
                qti[base + end:base + SLAB],
            ], axis=0)
            qt_out[0, base:base + SLAB, r] = slab.astype(BF16)
        kr = _rope_slab(zi[:, o_kr:o_kr + SLAB], ropek_ref[0, r], ropek_ref[1, r], ropek_ref[2, r]) + ones_lane
        for hd in range(N_HEADS):
            sl = slice(hd * SLAB, (hd + 1) * SLAB)
            k_out[0, r, sl] = (kki[:, sl] + kr).astype(BF16)
        for hd in range(N_HEADS):
            base = hd * V_SLAB
            vt_out[0, base:base + V_SLAB, r] = (vti[base:base + V_SLAB] + ones_row).astype(BF16)

    low = lax.broadcasted_iota(jnp.int32, (CHUNK, SLAB), 1) < C_B
    for r, zi in zip(blocks, z):
        gu = jax.nn.gelu(zi[:, o_u:o_v])
        vn = _rms(jax.nn.gelu(zi[:, o_v:o_kr]), gsgu_ref[...])
        n_chunks = (r.stop - r.start) // CHUNK
        pair = 2 if n_chunks % 2 == 0 else 1
        for c in range(0, n_chunks, pair):
            for j in range(D_B // SLAB):
                cols = slice(j * SLAB, (j + 1) * SLAB)
                rhs = []
                for cc in range(c, c + pair):
                    slab = vn[cc * CHUNK:(cc + 1) * CHUNK, cols]
                    rhs.append(jnp.concatenate([jnp.where(low, slab, 0.0), jnp.where(low, 0.0, slab)], axis=0))
                mixed = _dot(ws_ref[j], jnp.concatenate(rhs, axis=1).astype(BF16))
                for k, cc in enumerate(range(c, c + pair)):
                    out_rows = slice(r.start + cc * CHUNK, r.start + (cc + 1) * CHUNK)
                    mixed_sc[out_rows, cols] = mixed[:, k * SLAB:(k + 1) * SLAB] + bs_ref[:, cols]
        sgu = gu * mixed_sc[r]
        sg_out[0, r] = _rms(sgu, goutb_ref[...]).astype(BF16)


def _layer_spec(shape, l, **kw):
    return pl.BlockSpec((None,) + tuple(shape), lambda bi, i: (l,) + (0,) * len(shape), **kw)


def _mod_spec(l, row0):
    return pl.BlockSpec((None, 1, 1, N_MOD * D_MODEL), lambda bi, i: (l, row0 + bi, 0, 0))


def _premix(x, mod, p, l, row0, rope_k, rope_q):
    b, s, _ = x.shape
    t = _token_tile(s, 1024)
    tok = lambda bi, i: (bi, i, 0)
    tok_t = lambda bi, i: (bi, 0, i)
    n_slab = N_HEADS * SLAB
    layer = functools.partial(_layer_spec, l=l)
    return pl.pallas_call(
        _premix_kernel,
        out_shape=(
            jax.ShapeDtypeStruct((b, n_slab, s), BF16),
            jax.ShapeDtypeStruct((b, s, n_slab), BF16),
            jax.ShapeDtypeStruct((b, N_HEADS * V_SLAB, s), BF16),
            jax.ShapeDtypeStruct((b, s, D_B), BF16),
        ),
        grid=(b, s // t),
        in_specs=[
            pl.BlockSpec((1, t, D_MODEL), tok),
            _mod_spec(l, row0),
            layer((1, D_MODEL)),
            layer((D_MODEL, IN_COLS_PADDED)),
            layer((1, Q_LORA)),
            layer((n_slab, Q_LORA)),
            layer((1, KV_LORA)),
            layer((KV_LORA, n_slab)),
            layer((N_HEADS * V_SLAB, KV_LORA)),
            layer((1, D_B)),
            layer((D_B // SLAB, CHUNK, 2 * CHUNK)),
            layer((CHUNK, D_B)),
            layer((1, D_B)),
            pl.BlockSpec((3, t, SLAB), lambda bi, i: (0, i, 0)),
            pl.BlockSpec((2, ROPE_HALF, t), lambda bi, i: (0, 0, i)),
        ],
        out_specs=(
            pl.BlockSpec((1, n_slab, t), tok_t),
            pl.BlockSpec((1, t, n_slab), tok),
            pl.BlockSpec((1, N_HEADS * V_SLAB, t), tok_t),
            pl.BlockSpec((1, t, D_B), tok),
        ),
        scratch_shapes=[pltpu.VMEM((t, D_B), F32)],
        compiler_params=pltpu.CompilerParams(
            dimension_semantics=("arbitrary", "arbitrary"), vmem_limit_bytes=VMEM_LIMIT),
        name="premix",
    )(x, mod, p["g_pre_mix"], p["w_in"], p["g_q_a"], p["w_qt"], p["g_kv_a"], p["w_k"], p["w_vt"], p["g_sgu"],
      p["w_spatial"], p["b_spatial"], p["g_out_b"], rope_k, rope_q)


def _two_bf16(x):
    xn = x + jnp.abs(x) * 2.0 ** -15
    hi = xn.astype(BF16).astype(F32)
    lo = (xn - hi).astype(BF16).astype(F32)
    return jnp.concatenate([hi, lo], axis=0)


def _attn_kernel(qt_ref, k_ref, vt_ref, o_ref, qs_sc, m_sc, acc_sc, p_sc, *, tk):
    n_chunks = k_ref.shape[1] // tk
    tq = qs_sc.shape[-1]
    heads = range(HEADS_PER_STEP)
    pad_rows = jnp.zeros((STAB_ROWS - STAB_PARTS, tq), BF16)

    def set_stabiliser(hd, parts):
        qs_sc[hd, STAB_ROW:STAB_ROW + STAB_ROWS, :] = jnp.concatenate([(-parts).astype(BF16), pad_rows], axis=0)

    def chunk(c, src, dst, floor):
        start = pl.multiple_of(c * tk, tk)
        hi = lo = None
        for hd in heads:
            sl = slice(hd * SLAB, (hd + 1) * SLAB)
            parts = m_sc[src, hd]
            m_old = parts[0:1] + parts[1:2]
            st = _dot(k_ref[0, pl.ds(start, tk), sl], qs_sc[hd])
            mx = jnp.max(st, axis=0, keepdims=True)
            vt = vt_ref[0, hd * V_SLAB:(hd + 1) * V_SLAB, pl.ds(start, tk)]
            p_sc[hd % 2] = jnp.exp2(st).astype(BF16)
            pv = _dot(vt, p_sc[hd % 2])
            parts = jnp.where((mx > 0.0) | (mx < floor), _two_bf16(m_old + mx), parts)
            m_new = parts[0:1] + parts[1:2]
            acc_sc[dst, hd] = (acc_sc[src, hd] + pv) * jnp.exp2(jnp.minimum(m_old - m_new, 0.0))
            m_sc[dst, hd] = parts
            top, bot = jnp.max(mx), jnp.min(mx)
            hi = top if hi is None else jnp.maximum(hi, top)
            lo = bot if lo is None else jnp.minimum(lo, bot)
        return hi, lo

    def step(c, src, dst):
        floor = jnp.where(c == 0, -GROWTH_LIMIT, NEG_BIG)
        hi, lo = chunk(c, src, dst, floor)

        @pl.when((hi > GROWTH_LIMIT) | (lo < floor))
        def _():
            for hd in heads:
                new, old = m_sc[dst, hd], m_sc[src, hd]
                shift = (old[0:1] + old[1:2]) - (new[0:1] + new[1:2])
                acc_sc[src, hd] = acc_sc[src, hd] * jnp.exp2(jnp.minimum(shift, 0.0))
                m_sc[src, hd] = new
                set_stabiliser(hd, new)
            chunk(c, src, dst, NEG_BIG)

        for hd in heads:
            set_stabiliser(hd, m_sc[dst, hd])

    for hd in heads:
        qs_sc[hd] = qt_ref[0, hd * SLAB:(hd + 1) * SLAB, :]
        m_sc[0, hd] = jnp.zeros(m_sc.shape[2:], F32)
        acc_sc[0, hd] = jnp.zeros(acc_sc.shape[2:], F32)

    def body(c, carry):
        for par in range(2):
            @pl.when(c % 2 == par)
            def _():
                step(c, par, 1 - par)
        return carry

    lax.fori_loop(0, n_chunks, body, 0)
    last = n_chunks % 2
    outs = []
    for hd in heads:
        acc = acc_sc[last, hd]
        outs.append(acc[0:V_DIM] / acc[V_DIM:V_DIM + 1])
    o_ref[0] = jnp.concatenate(outs, axis=0).T.astype(o_ref.dtype)


def _attention(qt, k, vt):
    b, s, _ = k.shape
    tq = min(s, Q_TILE)
    tk = min(s, KV_CHUNK)
    assert s % tq == 0 and s % tk == 0 and tk % LANES == 0
    w = HEADS_PER_STEP * SLAB
    return pl.pallas_call(
        functools.partial(_attn_kernel, tk=tk),
        out_shape=jax.ShapeDtypeStruct((b, s, D_A), BF16),
        grid=(b, N_HEADS // HEADS_PER_STEP, s // tq),
        in_specs=[
            pl.BlockSpec((1, w, tq), lambda bi, hp, i: (bi, hp, i)),
            pl.BlockSpec((1, s, w), lambda bi, hp, i: (bi, 0, hp)),
            pl.BlockSpec((1, HEADS_PER_STEP * V_SLAB, s), lambda bi, hp, i: (bi, hp, 0)),
        ],
        out_specs=pl.BlockSpec((1, tq, HEADS_PER_STEP * V_DIM), lambda bi, hp, i: (bi, i, hp)),
        scratch_shapes=[
            pltpu.VMEM((HEADS_PER_STEP, SLAB, tq), BF16),
            pltpu.VMEM((2, HEADS_PER_STEP, STAB_PARTS, tq), F32),
            pltpu.VMEM((2, HEADS_PER_STEP, V_SLAB, tq), F32),
            pltpu.VMEM((2, tk, tq), BF16),
        ],
        compiler_params=pltpu.CompilerParams(
            dimension_semantics=("arbitrary", "arbitrary", "arbitrary"), vmem_limit_bytes=VMEM_LIMIT,
        ),
        name="mla_attention",
    )(qt, k, vt)


def _postmix_kernel(x_ref, mod_ref, attn_ref, sg_ref, gouta_ref, wout_ref, gpost_ref, gpre_ref, wgate_ref,
                    wup_ref, wdown_ref, gpostf_ref, o_ref):
    gate1 = mod_ref[0, :, 2 * D_MODEL:3 * D_MODEL]
    shift2 = mod_ref[0, :, 3 * D_MODEL:4 * D_MODEL]
    scale2 = mod_ref[0, :, 4 * D_MODEL:5 * D_MODEL]
    gate2 = mod_ref[0, :, 5 * D_MODEL:6 * D_MODEL]
    t = x_ref.shape[1]
    n_sub = t // SUBTILE_ROWS if t % SUBTILE_ROWS == 0 else 1
    blocks = [slice(i * t // n_sub, (i + 1) * t // n_sub) for i in range(n_sub)]
    merged = [jnp.concatenate([_rms(attn_ref[0, r].astype(F32), gouta_ref[...]).astype(BF16), sg_ref[0, r]], axis=1)
              for r in blocks]
    m = [_dot(v, wout_ref[...]) for v in merged]
    x1 = [x_ref[0, r] + gate1 * _rms(v, gpost_ref[...]) for r, v in zip(blocks, m)]
    h = [(_rms(v, gpre_ref[...]) * (1.0 + scale2) + shift2).astype(BF16) for v in x1]
    f = [None] * n_sub
    for lo, hi in zip(FF_PIECES[:-1], FF_PIECES[1:]):
        cols = slice(lo, hi)
        for i in range(n_sub):
            act = (jax.nn.silu(_dot(h[i], wgate_ref[:, cols])) * _dot(h[i], wup_ref[:, cols])).astype(BF16)
            part = _dot(act, wdown_ref[cols, :])
            f[i] = part if f[i] is None else f[i] + part
    for r, v1, vf in zip(blocks, x1, f):
        o_ref[0, r] = v1 + gate2 * _rms(vf, gpostf_ref[...])


def _postmix(x, mod, attn, sg, p, l, row0):
    b, s, _ = x.shape
    t = _token_tile(s, 1024)
    tok = lambda bi, i: (bi, i, 0)
    layer = functools.partial(_layer_spec, l=l)
    resident = functools.partial(_layer_spec, l=l, pipeline_mode=pl.Buffered(1))
    return pl.pallas_call(
        _postmix_kernel,
        out_shape=jax.ShapeDtypeStruct(x.shape, F32),
        grid=(b, s // t),
        in_specs=[
            pl.BlockSpec((1, t, D_MODEL), tok),
            _mod_spec(l, row0),
            pl.BlockSpec((1, t, D_A), tok),
            pl.BlockSpec((1, t, D_B), tok),
            layer((1, D_A)),
            resident((D_A + D_B, D_MODEL)),
            layer((1, D_MODEL)),
            layer((1, D_MODEL)),
            resident((D_MODEL, D_FF)),
            resident((D_MODEL, D_FF)),
            resident((D_FF, D_MODEL)),
            layer((1, D_MODEL)),
        ],
        out_specs=pl.BlockSpec((1, t, D_MODEL), tok),
        compiler_params=pltpu.CompilerParams(
            dimension_semantics=("arbitrary", "arbitrary"), vmem_limit_bytes=VMEM_LIMIT),
        name="postmix",
    )(x, mod, attn, sg, p["g_out_a"], p["w_out"], p["g_post_mix"], p["g_pre_ffn"], p["w_gate"], p["w_up"],
      p["w_down"], p["g_post_ffn"])


def _pad_heads(w, width, slab=SLAB):
    lead = w.shape[:-1]
    w = w.reshape(lead + (N_HEADS, width))
    return jnp.pad(w, ((0, 0),) * (len(lead) + 1) + ((0, slab - width),)).reshape(lead + (N_HEADS * slab,))


def _prep_params(w_in, g_q_a, w_q_b, g_kv_a, w_kv_b, g_sgu, w_spatial, b_spatial, g_out_a, g_out_b, w_out,
                 g_pre_mix, g_post_mix, g_pre_ffn, g_post_ffn, w_gate, w_up, w_down):
    w_in, w_q_b, w_kv_b, w_spatial = (w.astype(BF16) for w in (w_in, w_q_b, w_kv_b, w_spatial))
    o_kr = Q_LORA + KV_LORA
    o_u = o_kr + QK_ROPE
    kr_slab = jnp.pad(w_in[:, :, o_kr:o_u], ((0, 0), (0, 0), (QK_NOPE, SLAB - QK_NOPE - QK_ROPE)))
    w_in_p = jnp.concatenate([w_in[:, :, :o_kr], w_in[:, :, o_u:], kr_slab], axis=2)
    kvw = w_kv_b.reshape(DEPTH, KV_LORA, N_HEADS, QK_NOPE + V_DIM)
    w_k = _pad_heads(kvw[..., :QK_NOPE].reshape(DEPTH, KV_LORA, N_HEADS * QK_NOPE), QK_NOPE)
    w_v = _pad_heads(kvw[..., QK_NOPE:].reshape(DEPTH, KV_LORA, N_HEADS * V_DIM), V_DIM, V_SLAB)
    ws_pairs = jnp.concatenate([w_spatial[:, 0::2], w_spatial[:, 1::2]], axis=3)
    bias = jnp.repeat(jnp.swapaxes(b_spatial, 1, 2), C_B, axis=2)
    row = lambda g: g.reshape(DEPTH, 1, -1)
    t_bf16 = lambda w: jnp.swapaxes(w, 1, 2).astype(BF16)
    return dict(
        g_pre_mix=row(g_pre_mix), w_in=w_in_p.astype(BF16), g_q_a=row(g_q_a),
        w_qt=t_bf16(_pad_heads(w_q_b, QK_NOPE + QK_ROPE)), g_kv_a=row(g_kv_a),
        w_k=w_k.astype(BF16), w_vt=t_bf16(w_v), g_sgu=row(g_sgu),
        w_spatial=ws_pairs.astype(BF16), b_spatial=bias, g_out_b=row(g_out_b), g_out_a=row(g_out_a),
        w_out=w_out.astype(BF16), g_post_mix=row(g_post_mix), g_pre_ffn=row(g_pre_ffn),
        w_gate=w_gate.astype(BF16), w_up=w_up.astype(BF16), w_down=w_down.astype(BF16),
        g_post_ffn=row(g_post_ffn))


def _rope_tables(seq):
    pos = jnp.arange(seq, dtype=F32)
    inv_freq = ROPE_THETA ** (-jnp.arange(0, QK_ROPE, 2, dtype=F32) / QK_ROPE)
    ang = pos[:, None] * inv_freq[None, :]
    cos, sin = jnp.cos(ang), jnp.sin(ang)
    zeros = lambda n: jnp.zeros((seq, n), F32)
    tail = SLAB - QK_NOPE - QK_ROPE
    c = jnp.concatenate([jnp.ones((seq, QK_NOPE), F32), cos, cos, zeros(tail)], axis=1)
    s_lo = jnp.concatenate([zeros(QK_NOPE), -sin, zeros(ROPE_HALF + tail)], axis=1)
    s_hi = jnp.concatenate([zeros(QK_NOPE + ROPE_HALF), sin, zeros(tail)], axis=1)
    return jnp.stack([c, s_lo, s_hi]), jnp.stack([cos.T, sin.T])


def _trunk(x, mod, row0, params, rope_k, rope_q):
    for l in range(DEPTH):
        qt, k, vt, sg = _premix(x, mod, params, l, row0, rope_k, rope_q)
        attn = _attention(qt, k, vt)
        x = _postmix(x, mod, attn, sg, params, l, row0)
    return x


def kernel(x_prompt, x_sample, c_prompt, c_sample, w_mod, b_mod, g_pre_mix, g_post_mix, g_pre_ffn, g_post_ffn,
           w_in, g_q_a, w_q_b, g_kv_a, w_kv_b, g_sgu, w_spatial, b_spatial, g_out_a, g_out_b, w_out,
           w_gate, w_up, w_down):
    params = _prep_params(w_in, g_q_a, w_q_b, g_kv_a, w_kv_b, g_sgu, w_spatial, b_spatial, g_out_a, g_out_b, w_out,
                          g_pre_mix, g_post_mix, g_pre_ffn, g_post_ffn, w_gate, w_up, w_down)
    nb_p, nb_s = c_prompt.shape[0], c_sample.shape[0]
    rows = -(-(nb_p + nb_s) // 8) * 8
    c_all = jnp.concatenate([c_prompt, c_sample, jnp.zeros((rows - nb_p - nb_s, D_MODEL), F32)], axis=0)
    mod = _modulation(c_all, w_mod, b_mod).reshape(DEPTH, rows, 1, N_MOD * D_MODEL)
    rope_k, rope_q = _rope_tables(max(x_prompt.shape[1], x_sample.shape[1]))
    y_prompt = _trunk(x_prompt, mod, 0, params, rope_k, rope_q)
    y_sample = _trunk(x_sample, mod, nb_p, params, rope_k, rope_q)
    return (y_prompt, y_sample)
```

```python
import functools
import math

import jax
import jax.numpy as jnp
from jax import lax
from jax.experimental import pallas as pl
from jax.experimental.pallas import tpu as pltpu

D_MODEL = 1024
DEPTH = 2
N_HEADS = 8
QK_NOPE = 64
QK_ROPE = 32
ROPE_HALF = QK_ROPE // 2
V_DIM = 64
Q_LORA = 384
KV_LORA = 256
D_A = N_HEADS * V_DIM
D_B = 512
G_B = 8
C_B = D_B // G_B
CHUNK = 128
D_FF = 2816
EPS = 1e-6
N_MOD = 6
ROPE_THETA = 10000.0

LANES = 128
SLAB = LANES
BF16_ROWS = 16
V_SLAB = V_DIM + BF16_ROWS
HEADS_PER_STEP = 4
Q_TILE = 1024
KV_CHUNK = 2048
STAB_ROW = QK_NOPE + QK_ROPE
STAB_PARTS = 2
STAB_ROWS = BF16_ROWS
GROWTH_LIMIT = 64.0
IN_COLS_PADDED = Q_LORA + KV_LORA + 2 * D_B + SLAB
SUBTILE_ROWS = 256
MXU_DEPTH = 256
FF_PIECES = (0, 6 * MXU_DEPTH, D_FF)
VMEM_LIMIT = 56 * 1024 * 1024

BF16 = jnp.bfloat16
F32 = jnp.float32
NEG_BIG = -1e30
Q_SCALE = math.log2(math.e) / math.sqrt(QK_NOPE + QK_ROPE)


def _token_tile(seq, tile=512):
    return min(seq, tile)


def _rms(x, g):
    r = lax.rsqrt(jnp.mean(x * x, axis=-1, keepdims=True) + EPS)
    return x * r * g


def _dot(a, b):
    return jnp.dot(a, b, preferred_element_type=F32)


def _dot_nt(a, b):
    return lax.dot_general(a, b, (((1,), (1,)), ((), ())), preferred_element_type=F32)


def _mod_kernel(c_ref, w_ref, b_ref, o_ref):
    cs = jax.nn.silu(c_ref[...])
    o_ref[0] = _dot(cs.astype(BF16), w_ref[0].astype(BF16)) + b_ref[0]


def _modulation(c_all, w_mod, b_mod):
    rows = c_all.shape[0]
    n = w_mod.shape[-1]
    tn = 1536
    return pl.pallas_call(
        _mod_kernel,
        out_shape=jax.ShapeDtypeStruct((DEPTH, rows, n), F32),
        grid=(DEPTH, n // tn),
        in_specs=[
            pl.BlockSpec((rows, D_MODEL), lambda l, j: (0, 0)),
            pl.BlockSpec((1, D_MODEL, tn), lambda l, j: (l, 0, j)),
            pl.BlockSpec((1, 1, tn), lambda l, j: (l, 0, j)),
        ],
        out_specs=pl.BlockSpec((1, rows, tn), lambda l, j: (l, 0, j)),
        compiler_params=pltpu.CompilerParams(
            dimension_semantics=("arbitrary", "arbitrary"), vmem_limit_bytes=VMEM_LIMIT),
        name="adaln_modulation",
    )(c_all, w_mod, b_mod.reshape(DEPTH, 1, n))


def _rope_slab(x, c, s_lo, s_hi):
    return x * c + pltpu.roll(x, SLAB - ROPE_HALF, 1) * s_lo + pltpu.roll(x, ROPE_HALF, 1) * s_hi


def _premix_kernel(x_ref, mod_ref, gpre_ref, win_ref, gq_ref, wqt_ref, gkv_ref, wk_ref, wvt_ref, gsgu_ref, ws_ref,
                   bs_ref, goutb_ref, ropek_ref, ropeq_ref, qt_out, k_out, vt_out, sg_out, mixed_sc):
    shift = mod_ref[0, :, 0:D_MODEL]
    scale = mod_ref[0, :, D_MODEL:2 * D_MODEL]
    t = x_ref.shape[1]
    n_sub = t // SUBTILE_ROWS if t % SUBTILE_ROWS == 0 else 1
    blocks = [slice(i * t // n_sub, (i + 1) * t // n_sub) for i in range(n_sub)]
    o_kv, o_u, o_v, o_kr = Q_LORA, Q_LORA + KV_LORA, Q_LORA + KV_LORA + D_B, Q_LORA + KV_LORA + 2 * D_B

    z = []
    for r in blocks:
        h = _rms(x_ref[0, r], gpre_ref[...]) * (1.0 + scale) + shift
        z.append(_dot(h.astype(BF16), win_ref[...]))

    lo0, hi0, end = QK_NOPE, QK_NOPE + ROPE_HALF, QK_NOPE + QK_ROPE
    qt = [_dot_nt(wqt_ref[...], _rms(zi[:, 0:Q_LORA], gq_ref[...]).astype(BF16)) for zi in z]
    kvn = [_rms(zi[:, o_kv:o_u], gkv_ref[...]).astype(BF16) for zi in z]
    kk = [_dot(v, wk_ref[...]) for v in kvn]
    vt = [_dot_nt(wvt_ref[...], v) for v in kvn]
    lane = lax.broadcasted_iota(jnp.int32, (1, SLAB), 1)
    ones_lane = ((lane >= STAB_ROW) & (lane < STAB_ROW + STAB_PARTS)).astype(F32)
    ones_row = (lax.broadcasted_iota(jnp.int32, (V_SLAB, 1), 0) == V_DIM).astype(F32)
    for r, zi, qti, kki, vti in zip(blocks, z, qt, kk, vt):
        cos_t, sin_t = ropeq_ref[0, :, r], ropeq_ref[1, :, r]
        for hd in range(N_HEADS):
            base = hd * SLAB
            x_lo = qti[base + lo0:base + hi0]
            x_hi = qti[base + hi0:base + end]
            slab = jnp.concatenate([
                qti[base:base + lo0] * Q_SCALE,
                (x_lo * cos_t - x_hi * sin_t) * Q_SCALE,
                (x_lo * sin_t + x_hi * cos_t) * Q_SCALE,
                qti[base + end:base + SLAB],
            ], axis=0)
            qt_out[0, base:base + SLAB, r] = slab.astype(BF16)
        kr = _rope_slab(zi[:, o_kr:o_kr + SLAB], ropek_ref[0, r], ropek_ref[1, r], ropek_ref[2, r]) + ones_lane
        for hd in range(N_HEADS):
            sl = slice(hd * SLAB, (hd + 1) * SLAB)
            k_out[0, r, sl] = (kki[:, sl] + kr).astype(BF16)
        for hd in range(N_HEADS):
            base = hd * V_SLAB
            vt_out[0, base:base + V_SLAB, r] = (vti[base:base + V_SLAB] + ones_row).astype(BF16)

    low = lax.broadcasted_iota(jnp.int32, (CHUNK, SLAB), 1) < C_B
    for r, zi in zip(blocks, z):
        gu = jax.nn.gelu(zi[:, o_u:o_v])
        vn = _rms(jax.nn.gelu(zi[:, o_v:o_kr]), gsgu_ref[...])
        n_chunks = (r.stop - r.start) // CHUNK
        pair = 2 if n_chunks % 2 == 0 else 1
        for c in range(0, n_chunks, pair):
            for j in range(D_B // SLAB):
                cols = slice(j * SLAB, (j + 1) * SLAB)
                rhs = []
                for cc in range(c, c + pair):
                    slab = vn[cc * CHUNK:(cc + 1) * CHUNK, cols]
                    rhs.append(jnp.concatenate([jnp.where(low, slab, 0.0), jnp.where(low, 0.0, slab)], axis=0))
                mixed = _dot(ws_ref[j], jnp.concatenate(rhs, axis=1).astype(BF16))
                for k, cc in enumerate(range(c, c + pair)):
                    out_rows = slice(r.start + cc * CHUNK, r.start + (cc + 1) * CHUNK)
                    mixed_sc[out_rows, cols] = mixed[:, k * SLAB:(k + 1) * SLAB] + bs_ref[:, cols]
        sgu = gu * mixed_sc[r]
        sg_out[0, r] = _rms(sgu, goutb_ref[...]).astype(BF16)


def _layer_spec(shape, l, **kw):
    return pl.BlockSpec((None,) + tuple(shape), lambda bi, i: (l,) + (0,) * len(shape), **kw)


def _mod_spec(l, row0):
    return pl.BlockSpec((None, 1, 1, N_MOD * D_MODEL), lambda bi, i: (l, row0 + bi, 0, 0))


def _premix(x, mod, p, l, row0, rope_k, rope_q):
    b, s, _ = x.shape
    t = _token_tile(s, 1024)
    tok = lambda bi, i: (bi, i, 0)
    tok_t = lambda bi, i: (bi, 0, i)
    n_slab = N_HEADS * SLAB
    layer = functools.partial(_layer_spec, l=l)
    return pl.pallas_call(
        _premix_kernel,
        out_shape=(
            jax.ShapeDtypeStruct((b, n_slab, s), BF16),
            jax.ShapeDtypeStruct((b, s, n_slab), BF16),
            jax.ShapeDtypeStruct((b, N_HEADS * V_SLAB, s), BF16),
            jax.ShapeDtypeStruct((b, s, D_B), BF16),
        ),
        grid=(b, s // t),
        in_specs=[
            pl.BlockSpec((1, t, D_MODEL), tok),
            _mod_spec(l, row0),
            layer((1, D_MODEL)),
            layer((D_MODEL, IN_COLS_PADDED)),
            layer((1, Q_LORA)),
            layer((n_slab, Q_LORA)),
            layer((1, KV_LORA)),
            layer((KV_LORA, n_slab)),
            layer((N_HEADS * V_SLAB, KV_LORA)),
            layer((1, D_B)),
            layer((D_B // SLAB, CHUNK, 2 * CHUNK)),
            layer((CHUNK, D_B)),
            layer((1, D_B)),
            pl.BlockSpec((3, t, SLAB), lambda bi, i: (0, i, 0)),
            pl.BlockSpec((2, ROPE_HALF, t), lambda bi, i: (0, 0, i)),
        ],
        out_specs=(
            pl.BlockSpec((1, n_slab, t), tok_t),
            pl.BlockSpec((1, t, n_slab), tok),
            pl.BlockSpec((1, N_HEADS * V_SLAB, t), tok_t),
            pl.BlockSpec((1, t, D_B), tok),
        ),
        scratch_shapes=[pltpu.VMEM((t, D_B), F32)],
        compiler_params=pltpu.CompilerParams(
            dimension_semantics=("arbitrary", "arbitrary"), vmem_limit_bytes=VMEM_LIMIT),
        name="premix",
    )(x, mod, p["g_pre_mix"], p["w_in"], p["g_q_a"], p["w_qt"], p["g_kv_a"], p["w_k"], p["w_vt"], p["g_sgu"],
      p["w_spatial"], p["b_spatial"], p["g_out_b"], rope_k, rope_q)


def _two_bf16(x):
    xn = x + jnp.abs(x) * 2.0 ** -15
    hi = xn.astype(BF16).astype(F32)
    lo = (xn - hi).astype(BF16).astype(F32)
    return jnp.concatenate([hi, lo], axis=0)


def _attn_kernel(qt_ref, k_ref, vt_ref, o_ref, qs_sc, m_sc, acc_sc, p_sc, *, tk):
    n_chunks = k_ref.shape[1] // tk
    tq = qs_sc.shape[-1]
    heads = range(HEADS_PER_STEP)
    pad_rows = jnp.zeros((STAB_ROWS - STAB_PARTS, tq), BF16)

    def set_stabiliser(hd, parts):
        qs_sc[hd, STAB_ROW:STAB_ROW + STAB_ROWS, :] = jnp.concatenate([(-parts).astype(BF16), pad_rows], axis=0)

    def chunk(c, src, dst, floor):
        start = pl.multiple_of(c * tk, tk)
        hi = lo = None
        for hd in heads:
            sl = slice(hd * SLAB, (hd + 1) * SLAB)
            parts = m_sc[src, hd]
            m_old = parts[0:1] + parts[1:2]
            st = _dot(k_ref[0, pl.ds(start, tk), sl], qs_sc[hd])
            mx = jnp.max(st, axis=0, keepdims=True)
            vt = vt_ref[0, hd * V_SLAB:(hd + 1) * V_SLAB, pl.ds(start, tk)]
            p_sc[hd % 2] = jnp.exp2(st).astype(BF16)
            pv = _dot(vt, p_sc[hd % 2])
            parts = jnp.where((mx > 0.0) | (mx < floor), _two_bf16(m_old + mx), parts)
            m_new = parts[0:1] + parts[1:2]
            acc_sc[dst, hd] = (acc_sc[src, hd] + pv) * jnp.exp2(jnp.minimum(m_old - m_new, 0.0))
            m_sc[dst, hd] = parts
            top, bot = jnp.max(mx), jnp.min(mx)
            hi = top if hi is None else jnp.maximum(hi, top)
            lo = bot if lo is None else jnp.minimum(lo, bot)
        return hi, lo

    def step(c, src, dst):
        floor = jnp.where(c == 0, -GROWTH_LIMIT, NEG_BIG)
        hi, lo = chunk(c, src, dst, floor)

        @pl.when((hi > GROWTH_LIMIT) | (lo < floor))
        def _():
            for hd in heads:
                new, old = m_sc[dst, hd], m_sc[src, hd]
                shift = (old[0:1] + old[1:2]) - (new[0:1] + new[1:2])
                acc_sc[src, hd] = acc_sc[src, hd] * jnp.exp2(jnp.minimum(shift, 0.0))
                m_sc[src, hd] = new
                set_stabiliser(hd, new)
            chunk(c, src, dst, NEG_BIG)

        for hd in heads:
            set_stabiliser(hd, m_sc[dst, hd])

    for hd in heads:
        qs_sc[hd] = qt_ref[0, hd * SLAB:(hd + 1) * SLAB, :]
        m_sc[0, hd] = jnp.zeros(m_sc.shape[2:], F32)
        acc_sc[0, hd] = jnp.zeros(acc_sc.shape[2:], F32)

    def body(c, carry):
        for par in range(2):
            @pl.when(c % 2 == par)
            def _():
                step(c, par, 1 - par)
        return carry

    lax.fori_loop(0, n_chunks, body, 0)
    last = n_chunks % 2
    outs = []
    for hd in heads:
        acc = acc_sc[last, hd]
        outs.append(acc[0:V_DIM] / acc[V_DIM:V_DIM + 1])
    o_ref[0] = jnp.concatenate(outs, axis=0).T.astype(o_ref.dtype)


def _attention(qt, k, vt):
    b, s, _ = k.shape
    tq = min(s, Q_TILE)
    tk = min(s, KV_CHUNK)
    assert s % tq == 0 and s % tk == 0 and tk % LANES == 0
    w = HEADS_PER_STEP * SLAB
    return pl.pallas_call(
        functools.partial(_attn_kernel, tk=tk),
        out_shape=jax.ShapeDtypeStruct((b, s, D_A), BF16),
        grid=(b, N_HEADS // HEADS_PER_STEP, s // tq),
        in_specs=[
            pl.BlockSpec((1, w, tq), lambda bi, hp, i: (bi, hp, i)),
            pl.BlockSpec((1, s, w), lambda bi, hp, i: (bi, 0, hp)),
            pl.BlockSpec((1, HEADS_PER_STEP * V_SLAB, s), lambda bi, hp, i: (bi, hp, 0)),
        ],
        out_specs=pl.BlockSpec((1, tq, HEADS_PER_STEP * V_DIM), lambda bi, hp, i: (bi, i, hp)),
        scratch_shapes=[
            pltpu.VMEM((HEADS_PER_STEP, SLAB, tq), BF16),
            pltpu.VMEM((2, HEADS_PER_STEP, STAB_PARTS, tq), F32),
            pltpu.VMEM((2, HEADS_PER_STEP, V_SLAB, tq), F32),
            pltpu.VMEM((2, tk, tq), BF16),
        ],
        compiler_params=pltpu.CompilerParams(
            dimension_semantics=("arbitrary", "arbitrary", "arbitrary"), vmem_limit_bytes=VMEM_LIMIT,
        ),
        name="mla_attention",
    )(qt, k, vt)


def _postmix_kernel(x_ref, mod_ref, attn_ref, sg_ref, gouta_ref, wout_ref, gpost_ref, gpre_ref, wgate_ref,
                    wup_ref, wdown_ref, gpostf_ref, o_ref):
    gate1 = mod_ref[0, :, 2 * D_MODEL:3 * D_MODEL]
    shift2 = mod_ref[0, :, 3 * D_MODEL:4 * D_MODEL]
    scale2 = mod_ref[0, :, 4 * D_MODEL:5 * D_MODEL]
    gate2 = mod_ref[0, :, 5 * D_MODEL:6 * D_MODEL]
    t = x_ref.shape[1]
    n_sub = t // SUBTILE_ROWS if t % SUBTILE_ROWS == 0 else 1
    blocks = [slice(i * t // n_sub, (i + 1) * t // n_sub) for i in range(n_sub)]
    merged = [jnp.concatenate([_rms(attn_ref[0, r].astype(F32), gouta_ref[...]).astype(BF16), sg_ref[0, r]], axis=1)
              for r in blocks]
    m = [_dot(v, wout_ref[...]) for v in merged]
    x1 = [x_ref[0, r] + gate1 * _rms(v, gpost_ref[...]) for r, v in zip(blocks, m)]
    h = [(_rms(v, gpre_ref[...]) * (1.0 + scale2) + shift2).astype(BF16) for v in x1]
    f = [None] * n_sub
    for lo, hi in zip(FF_PIECES[:-1], FF_PIECES[1:]):
        cols = slice(lo, hi)
        for i in range(n_sub):
            act = (jax.nn.silu(_dot(h[i], wgate_ref[:, cols])) * _dot(h[i], wup_ref[:, cols])).astype(BF16)
            part = _dot(act, wdown_ref[cols, :])
            f[i] = part if f[i] is None else f[i] + part
    for r, v1, vf in zip(blocks, x1, f):
        o_ref[0, r] = v1 + gate2 * _rms(vf, gpostf_ref[...])


def _postmix(x, mod, attn, sg, p, l, row0):
    b, s, _ = x.shape
    t = _token_tile(s, 1024)
    tok = lambda bi, i: (bi, i, 0)
    layer = functools.partial(_layer_spec, l=l)
    resident = functools.partial(_layer_spec, l=l, pipeline_mode=pl.Buffered(1))
    return pl.pallas_call(
        _postmix_kernel,
        out_shape=jax.ShapeDtypeStruct(x.shape, F32),
        grid=(b, s // t),
        in_specs=[
            pl.BlockSpec((1, t, D_MODEL), tok),
            _mod_spec(l, row0),
            pl.BlockSpec((1, t, D_A), tok),
            pl.BlockSpec((1, t, D_B), tok),
            layer((1, D_A)),
            resident((D_A + D_B, D_MODEL)),
            layer((1, D_MODEL)),
            layer((1, D_MODEL)),
            resident((D_MODEL, D_FF)),
            resident((D_MODEL, D_FF)),
            resident((D_FF, D_MODEL)),
            layer((1, D_MODEL)),
        ],
        out_specs=pl.BlockSpec((1, t, D_MODEL), tok),
        compiler_params=pltpu.CompilerParams(
            dimension_semantics=("arbitrary", "arbitrary"), vmem_limit_bytes=VMEM_LIMIT,
            allow_input_fusion=[i in (5, 8, 9, 10) for i in range(12)]),
        name="postmix",
    )(x, mod, attn, sg, p["g_out_a"], p["w_out"], p["g_post_mix"], p["g_pre_ffn"], p["w_gate"], p["w_up"],
      p["w_down"], p["g_post_ffn"])


def _pad_heads(w, width, slab=SLAB):
    lead = w.shape[:-1]
    w = w.reshape(lead + (N_HEADS, width))
    return jnp.pad(w, ((0, 0),) * (len(lead) + 1) + ((0, slab - width),)).reshape(lead + (N_HEADS * slab,))


def _prep_params(w_in, g_q_a, w_q_b, g_kv_a, w_kv_b, g_sgu, w_spatial, b_spatial, g_out_a, g_out_b, w_out,
                 g_pre_mix, g_post_mix, g_pre_ffn, g_post_ffn, w_gate, w_up, w_down):
    w_in, w_q_b, w_kv_b, w_spatial = (w.astype(BF16) for w in (w_in, w_q_b, w_kv_b, w_spatial))
    o_kr = Q_LORA + KV_LORA
    o_u = o_kr + QK_ROPE
    kr_slab = jnp.pad(w_in[:, :, o_kr:o_u], ((0, 0), (0, 0), (QK_NOPE, SLAB - QK_NOPE - QK_ROPE)))
    w_in_p = jnp.concatenate([w_in[:, :, :o_kr], w_in[:, :, o_u:], kr_slab], axis=2)
    kvw = w_kv_b.reshape(DEPTH, KV_LORA, N_HEADS, QK_NOPE + V_DIM)
    w_k = _pad_heads(kvw[..., :QK_NOPE].reshape(DEPTH, KV_LORA, N_HEADS * QK_NOPE), QK_NOPE)
    w_v = _pad_heads(kvw[..., QK_NOPE:].reshape(DEPTH, KV_LORA, N_HEADS * V_DIM), V_DIM, V_SLAB)
    ws_pairs = jnp.concatenate([w_spatial[:, 0::2], w_spatial[:, 1::2]], axis=3)
    bias = jnp.repeat(jnp.swapaxes(b_spatial, 1, 2), C_B, axis=2)
    row = lambda g: g.reshape(DEPTH, 1, -1)
    t_bf16 = lambda w: jnp.swapaxes(w, 1, 2).astype(BF16)
    return dict(
        g_pre_mix=row(g_pre_mix), w_in=w_in_p.astype(BF16), g_q_a=row(g_q_a),
        w_qt=t_bf16(_pad_heads(w_q_b, QK_NOPE + QK_ROPE)), g_kv_a=row(g_kv_a),
        w_k=w_k.astype(BF16), w_vt=t_bf16(w_v), g_sgu=row(g_sgu),
        w_spatial=ws_pairs.astype(BF16), b_spatial=bias, g_out_b=row(g_out_b), g_out_a=row(g_out_a),
        w_out=w_out.astype(BF16), g_post_mix=row(g_post_mix), g_pre_ffn=row(g_pre_ffn),
        w_gate=w_gate.astype(BF16), w_up=w_up.astype(BF16), w_down=w_down.astype(BF16),
        g_post_ffn=row(g_post_ffn))


def _rope_tables(seq):
    pos = jnp.arange(seq, dtype=F32)
    inv_freq = ROPE_THETA ** (-jnp.arange(0, QK_ROPE, 2, dtype=F32) / QK_ROPE)
    ang = pos[:, None] * inv_freq[None, :]
    cos, sin = jnp.cos(ang), jnp.sin(ang)
    zeros = lambda n: jnp.zeros((seq, n), F32)
    tail = SLAB - QK_NOPE - QK_ROPE
    c = jnp.concatenate([jnp.ones((seq, QK_NOPE), F32), cos, cos, zeros(tail)], axis=1)
    s_lo = jnp.concatenate([zeros(QK_NOPE), -sin, zeros(ROPE_HALF + tail)], axis=1)
    s_hi = jnp.concatenate([zeros(QK_NOPE + ROPE_HALF), sin, zeros(tail)], axis=1)
    return jnp.stack([c, s_lo, s_hi]), jnp.stack([cos.T, sin.T])


def _trunk(x, mod, row0, params, rope_k, rope_q):
    for l in range(DEPTH):
        qt, k, vt, sg = _premix(x, mod, params, l, row0, rope_k, rope_q)
        attn = _attention(qt, k, vt)
        x = _postmix(x, mod, attn, sg, params, l, row0)
    return x


def kernel(x_prompt, x_sample, c_prompt, c_sample, w_mod, b_mod, g_pre_mix, g_post_mix, g_pre_ffn, g_post_ffn,
           w_in, g_q_a, w_q_b, g_kv_a, w_kv_b, g_sgu, w_spatial, b_spatial, g_out_a, g_out_b, w_out,
           w_gate, w_up, w_down):
    params = _prep_params(w_in, g_q_a, w_q_b, g_kv_a, w_kv_b, g_sgu, w_spatial, b_spatial, g_out_a, g_out_b, w_out,
                          g_pre_mix, g_post_mix, g_pre_ffn, g_post_ffn, w_gate, w_up, w_down)
    nb_p, nb_s = c_prompt.shape[0], c_sample.shape[0]
    rows = -(-(nb_p + nb_s) // 8) * 8
    c_all = jnp.concatenate([c_prompt, c_sample, jnp.zeros((rows - nb_p - nb_s, D_MODEL), F32)], axis=0)
    mod = _modulation(c_all, w_mod, b_mod).reshape(DEPTH, rows, 1, N_MOD * D_MODEL)
    rope_k, rope_q = _rope_tables(max(x_prompt.shape[1], x_sample.shape[1]))
    y_prompt = _trunk(x_prompt, mod, 0, params, rope_k, rope_q)
    y_sample = _trunk(x_sample, mod, nb_p, params, rope_k, rope_q)
    return (y_prompt, y_sample)
```
